```python
import jax, jax.numpy as jnp
from jax import lax
import numpy as np

D_MODEL = 1024
BATCH = 8
SEQ = 2048
DEPTH = 1
DEC_BATCH = 128
DEC_SEQ = 1
PAST_LEN = 16384
PAGE_SIZE = 128

D_CONV = D_MODEL
CONV_A_W = 3
N_HEADS = 8
HEAD_K = 128
HEAD_V = 128
D_K = N_HEADS * HEAD_K
D_V = N_HEADS * HEAD_V
D_QKV = 2 * D_K + D_V
CONV_B_W = 4
CHUNK = 64
EPS = 1e-6
IN_SIZES = (D_CONV, D_CONV, D_CONV, D_CONV, D_QKV, D_V, N_HEADS, N_HEADS, D_MODEL, D_MODEL)
IN_COLS = sum(IN_SIZES)

kernel_name = "hybrid_shortconv_gdn_decode_step"


def rmsnorm(x, w):
    xf = x.astype(jnp.float32)
    y = xf * lax.rsqrt(jnp.mean(xf * xf, axis=-1, keepdims=True) + EPS)
    return (y * w.astype(jnp.float32)).astype(x.dtype)


def l2norm(x):
    xf = x.astype(jnp.float32)
    return xf * lax.rsqrt(jnp.sum(xf * xf, axis=-1, keepdims=True) + EPS)


def causal_dwconv(x, buf, w):
    width = w.shape[0]
    L = x.shape[1]
    xp = jnp.concatenate([buf.astype(x.dtype), x], axis=1)
    y = xp[:, 0:L] * w[0]
    for j in range(1, width):
        y = y + xp[:, j:j + L] * w[j]
    return y, xp[:, L:]


def gated_delta_chunked(q, k, v, beta, g, s0):
    bsz, L = q.shape[0], q.shape[1]
    dv = v.shape[-1]
    C = min(CHUNK, L)
    n = -(-L // C)
    pad = n * C - L

    def prep(t):
        t = t.astype(jnp.float32)
        t = jnp.pad(t, [(0, 0), (0, pad)] + [(0, 0)] * (t.ndim - 2))
        t = t.reshape((bsz, n, C) + t.shape[2:])
        return jnp.moveaxis(t, 3, 1)

    q, k, v, beta, g = prep(q), prep(k), prep(v), prep(beta), prep(g)
    q = q * (HEAD_K ** -0.5)
    gc = jnp.cumsum(g, axis=-1)
    idx = jnp.arange(C)
    causal = idx[:, None] >= idx[None, :]
    strict = idx[:, None] > idx[None, :]
    decay = jnp.exp(jnp.where(causal, gc[..., :, None] - gc[..., None, :], -jnp.inf))
    kb = k * beta[..., None]
    a_mat = jnp.where(strict, jnp.einsum('bhncd,bhnsd->bhncs', kb, k) * decay, 0.0)
    a_mat = a_mat + jnp.eye(C, dtype=jnp.float32)
    rhs = jnp.concatenate([v * beta[..., None], kb * jnp.exp(gc)[..., None]], axis=-1)
    sol = lax.linalg.triangular_solve(a_mat, rhs, left_side=True, lower=True, unit_diagonal=True)
    u, w = sol[..., :dv], sol[..., dv:]
    attn_qk = jnp.einsum('bhncd,bhnsd->bhncs', q, k) * decay
    q_dec = q * jnp.exp(gc)[..., None]
    k_dec = k * jnp.exp(gc[..., -1:] - gc)[..., None]
    g_last = jnp.exp(gc[..., -1])

    def step(S, xs):
        u_c, w_c, qd_c, aqk_c, kd_c, gl_c = xs
        v_new = u_c - jnp.einsum('bhcd,bhde->bhce', w_c, S)
        o = jnp.einsum('bhcd,bhde->bhce', qd_c, S) + jnp.einsum('bhcs,bhse->bhce', aqk_c, v_new)
        S = S * gl_c[..., None, None] + jnp.einsum('bhcd,bhce->bhde', kd_c, v_new)
        return S, o

    xs = tuple(jnp.moveaxis(t, 2, 0) for t in (u, w, q_dec, attn_qk, k_dec, g_last))
    S, o = lax.scan(step, s0.astype(jnp.float32), xs)
    o = jnp.moveaxis(o, 0, 2).reshape(bsz, o.shape[2], n * C, dv)[:, :, :L]
    return jnp.transpose(o, (0, 2, 1, 3)), S


def mixer_layer(x, c, buf_a, buf_b, s0, ada_w, ada_b, norm_pre, w_in, conv_a_w, conv_b_w,
                a_log, dt_bias, onorm_w, w_out_a, w_out_b, w_o, norm_post):
    bsz, L, _ = x.shape
    mod = jax.nn.silu(c) @ ada_w + ada_b
    shift, scale, gate = jnp.split(mod, 3, axis=-1)
    h = rmsnorm(x, norm_pre) * (1.0 + scale[:, None]) + shift[:, None]
    proj = h @ w_in
    offs = [int(o) for o in np.cumsum(IN_SIZES)[:-1]]
    hA, cA, bA, zA, qkv, zB, beta_raw, alpha_raw, gA, gB = jnp.split(proj, offs, axis=-1)

    a_conv, new_buf_a = causal_dwconv(cA * hA, buf_a, conv_a_w)
    yA = ((bA * a_conv) * jax.nn.silu(zA)) @ w_out_a

    qkv_c, new_buf_b = causal_dwconv(qkv, buf_b, conv_b_w)
    qkv_c = jax.nn.silu(qkv_c)
    q, k, v = jnp.split(qkv_c, [D_K, 2 * D_K], axis=-1)
    q = l2norm(q.reshape(bsz, L, N_HEADS, HEAD_K))
    k = l2norm(k.reshape(bsz, L, N_HEADS, HEAD_K))
    v = v.reshape(bsz, L, N_HEADS, HEAD_V)
    beta = jax.nn.sigmoid(beta_raw.astype(jnp.float32))
    g = -jnp.exp(a_log.astype(jnp.float32)) * jax.nn.softplus(
        alpha_raw.astype(jnp.float32) + dt_bias.astype(jnp.float32))
    o, S = gated_delta_chunked(q, k, v, beta, g, s0)
    o = rmsnorm(o, onorm_w) * jax.nn.silu(zB.reshape(bsz, L, N_HEADS, HEAD_V).astype(jnp.float32))
    yB = o.reshape(bsz, L, D_V).astype(x.dtype) @ w_out_b

    merged = jax.nn.sigmoid(gA) * yA + jax.nn.sigmoid(gB) * yB
    post = rmsnorm(merged @ w_o, norm_post)
    y = (x + gate[:, None] * post).astype(x.dtype)
    return y, new_buf_a, new_buf_b, S


def setup_inputs(seed: int = 0) -> dict:
    key = jax.random.key(seed)
    ks = jax.random.split(key, 24)
    f32 = jnp.float32

    def nrm(k, shape, s):
        return jax.random.normal(k, shape, f32) * s

    dt = jnp.exp(jax.random.uniform(ks[14], (DEPTH, N_HEADS), f32, np.log(1e-3), np.log(1e-1)))
    dt_bias = dt + jnp.log(-jnp.expm1(-dt))
    return {
        "x_prompt": nrm(ks[0], (BATCH, SEQ, D_MODEL), 1.0),
        "x_sample": nrm(ks[1], (DEC_BATCH, DEC_SEQ, D_MODEL), 1.0),
        "c_prompt": nrm(ks[2], (BATCH, D_MODEL), 1.0),
        "c_sample": nrm(ks[3], (DEC_BATCH, D_MODEL), 1.0),
        "state_conv_a": nrm(ks[4], (DEPTH, DEC_BATCH, CONV_A_W - 1, D_CONV), 1.0),
        "state_conv_qkv": nrm(ks[5], (DEPTH, DEC_BATCH, CONV_B_W - 1, D_QKV), 1.0),
        "state_delta": nrm(ks[6], (DEPTH, DEC_BATCH, N_HEADS, HEAD_K, HEAD_V), 0.1),
        "ada_w": nrm(ks[7], (DEPTH, D_MODEL, 3 * D_MODEL), 0.5 * D_MODEL ** -0.5),
        "ada_b": nrm(ks[8], (DEPTH, 3 * D_MODEL), 0.01),
        "norm_pre": 1.0 + nrm(ks[9], (DEPTH, D_MODEL), 0.05),
        "w_in": nrm(ks[10], (DEPTH, D_MODEL, IN_COLS), D_MODEL ** -0.5),
        "conv_a_w": nrm(ks[11], (DEPTH, CONV_A_W, D_CONV), CONV_A_W ** -0.5),
        "conv_b_w": nrm(ks[12], (DEPTH, CONV_B_W, D_QKV), CONV_B_W ** -0.5),
        "a_log": jnp.log(jax.random.uniform(ks[13], (DEPTH, N_HEADS), f32, 1.0, 16.0)),
        "dt_bias": dt_bias,
        "onorm_w": 1.0 + nrm(ks[15], (DEPTH, HEAD_V), 0.05),
        "w_out_a": nrm(ks[16], (DEPTH, D_CONV, D_MODEL), D_CONV ** -0.5),
        "w_out_b": nrm(ks[17], (DEPTH, D_V, D_MODEL), D_V ** -0.5),
        "w_o": nrm(ks[18], (DEPTH, D_MODEL, D_MODEL), D_MODEL ** -0.5),
        "norm_post": 1.0 + nrm(ks[19], (DEPTH, D_MODEL), 0.05),
    }


def reference(x_prompt, x_sample, c_prompt, c_sample, state_conv_a, state_conv_qkv, state_delta,
              ada_w, ada_b, norm_pre, w_in, conv_a_w, conv_b_w, a_log, dt_bias, onorm_w,
              w_out_a, w_out_b, w_o, norm_post):
    yp, ys = x_prompt, x_sample
    bp = x_prompt.shape[0]
    pa, pb, pd, sa, sb, sd = [], [], [], [], [], []
    for l in range(DEPTH):
        weights = (ada_w[l], ada_b[l], norm_pre[l], w_in[l], conv_a_w[l], conv_b_w[l], a_log[l],
                   dt_bias[l], onorm_w[l], w_out_a[l], w_out_b[l], w_o[l], norm_post[l])
        zero_a = jnp.zeros((bp, CONV_A_W - 1, D_CONV), yp.dtype)
        zero_b = jnp.zeros((bp, CONV_B_W - 1, D_QKV), yp.dtype)
        zero_s = jnp.zeros((bp, N_HEADS, HEAD_K, HEAD_V), jnp.float32)
        yp, na, nb, ns = mixer_layer(yp, c_prompt, zero_a, zero_b, zero_s, *weights)
        pa.append(na)
        pb.append(nb)
        pd.append(ns.astype(state_delta.dtype))
        ys, ma, mb, ms = mixer_layer(ys, c_sample, state_conv_a[l], state_conv_qkv[l], state_delta[l], *weights)
        sa.append(ma)
        sb.append(mb)
        sd.append(ms.astype(state_delta.dtype))
    return (yp, ys, jnp.stack(pa), jnp.stack(pb), jnp.stack(pd), jnp.stack(sa), jnp.stack(sb), jnp.stack(sd))
```

```python
import functools

import jax
import jax.numpy as jnp
from jax import lax
from jax.experimental import pallas as pl
from jax.experimental.pallas import tpu as pltpu

F32 = jnp.float32
BF16 = jnp.bfloat16

EPS = 1e-6
N_HEADS = 8
HEAD = 128
CONV_A_W = 3
CONV_B_W = 4

V7X_LANES = 128
V7X_SUBLANES = 8
V7X_VMEM_LIMIT_BYTES = 60 * 1024 * 1024

PROMPT_TILE = 256
CHUNK = 64
SAMPLE_BLOCK = 8

HI = lax.Precision.HIGHEST


def _sigmoid(x):
    return 1.0 / (1.0 + jnp.exp(-x))


def _silu(x):
    return x * _sigmoid(x)


def _softplus(x):
    return jnp.maximum(x, 0.0) + jnp.log1p(jnp.exp(-jnp.abs(x)))


def _dot(a, b):
    return jnp.dot(a.astype(BF16), b.astype(BF16), preferred_element_type=F32)


def _dot_nt(a, b):
    return lax.dot_general(a.astype(BF16), b.astype(BF16), (((1,), (1,)), ((), ())),
                           preferred_element_type=F32)


def _dot_tn(a, b):
    return lax.dot_general(a.astype(BF16), b.astype(BF16), (((0,), (0,)), ((), ())),
                           preferred_element_type=F32)


def _dot_exact(a, b):
    return jnp.dot(a, b, precision=HI, preferred_element_type=F32)


def _rms(x, w):
    return x * lax.rsqrt(jnp.mean(x * x, axis=-1, keepdims=True) + EPS) * w


def _head_l2norm(x, scale):
    outs = []
    for h in range(N_HEADS):
        xh = x[:, h * HEAD:(h + 1) * HEAD]
        outs.append(xh * (lax.rsqrt(jnp.sum(xh * xh, axis=-1, keepdims=True) + EPS) * scale))
    return jnp.concatenate(outs, axis=-1)


def _head_rms_gate(o, w, z):
    outs = []
    for h in range(N_HEADS):
        sl = slice(h * HEAD, (h + 1) * HEAD)
        oh = o[:, sl]
        outs.append(oh * lax.rsqrt(jnp.mean(oh * oh, axis=-1, keepdims=True) + EPS) * w * _silu(z[:, sl]))
    return jnp.concatenate(outs, axis=-1)


def _lane_bcast(x, col, width):
    return jnp.broadcast_to(x[:, col:col + 1], (x.shape[0], width))


def _adaln_kernel(c_ref, w_ref, b_ref, o_ref):
    o_ref[...] = _dot(_silu(c_ref[...]), w_ref[...]) + b_ref[...]


def _adaln(c_all, ada_w, ada_b):
    n, d = c_all.shape
    cols = ada_w.shape[1]
    return pl.pallas_call(
        _adaln_kernel,
        grid=(cols // d,),
        in_specs=[pl.BlockSpec((n, d), lambda j: (0, 0)),
                  pl.BlockSpec((d, d), lambda j: (0, j)),
                  pl.BlockSpec((1, d), lambda j: (0, j))],
        out_specs=pl.BlockSpec((n, d), lambda j: (0, j)),
        out_shape=jax.ShapeDtypeStruct((n, cols), F32),
        name="adaln",
    )(c_all, ada_w, ada_b)


def _prompt_kernel(x_ref, shift_ref, scale_ref, gate_ref, npre_ref, wmain_ref, wsmall_ref, wsmallt_ref,
                   caw_ref, cbw_ref, alc_ref, dtc_ref, alr_ref, dtr_ref, onw_ref, woa_ref, wob_ref, wo_ref,
                   npost_ref,
                   y_ref, ca_ref, cb_ref, s_ref,
                   abuf, qkvbuf, s_scr, q_scr, k_scr, v_scr, o_scr, b_scr, gc_scr, gt_scr, gcr_scr):
    lt = x_ref.shape[1]
    d = x_ref.shape[2]
    dqkv = qkvbuf.shape[1]
    n_chunks = lt // CHUNK
    pad = V7X_SUBLANES
    t = pl.program_id(1)

    @pl.when(t == 0)
    def _():
        abuf[0:pad, :] = jnp.zeros((pad, d), F32)
        qkvbuf[0:pad, :] = jnp.zeros((pad, dqkv), F32)
        s_scr[...] = jnp.zeros(s_scr.shape, F32)

    x = x_ref[0]
    h = _rms(x, npre_ref[...]) * (1.0 + scale_ref[0]) + shift_ref[0]
    hb = h.astype(BF16)

    pa = jnp.dot(hb, wmain_ref[:, 0:4 * d], preferred_element_type=F32)
    u = pa[:, d:2 * d] * pa[:, 0:d]
    abuf[pad:pad + lt, :] = u
    conv = u * caw_ref[CONV_A_W - 1:CONV_A_W, :]
    for j in range(CONV_A_W - 1):
        conv = conv + abuf[pad - (CONV_A_W - 1) + j:pad - (CONV_A_W - 1) + j + lt, :] * caw_ref[j:j + 1, :]
    ya = jnp.dot(((pa[:, 2 * d:3 * d] * conv) * _silu(pa[:, 3 * d:4 * d])).astype(BF16), woa_ref[...],
                 preferred_element_type=F32)
    ca_ref[0] = abuf[pad + lt - (CONV_A_W - 1):pad + lt, :]
    abuf[0:pad, :] = abuf[lt:lt + pad, :]

    pq = jnp.dot(hb, wmain_ref[:, 4 * d:4 * d + dqkv], preferred_element_type=F32)
    qkvbuf[pad:pad + lt, :] = pq
    cq = pq * cbw_ref[CONV_B_W - 1:CONV_B_W, :]
    for j in range(CONV_B_W - 1):
        cq = cq + qkvbuf[pad - (CONV_B_W - 1) + j:pad - (CONV_B_W - 1) + j + lt, :] * cbw_ref[j:j + 1, :]
    cq = _silu(cq)
    cb_ref[0] = qkvbuf[pad + lt - (CONV_B_W - 1):pad + lt, :]
    qkvbuf[0:pad, :] = qkvbuf[lt:lt + pad, :]
    q_scr[...] = _head_l2norm(cq[:, 0:d], HEAD ** -0.5)
    k_scr[...] = _head_l2norm(cq[:, d:2 * d], 1.0)
    v_scr[...] = cq[:, 2 * d:3 * d]

    ps = jnp.dot(hb, wsmall_ref[...], preferred_element_type=F32)
    pst = lax.dot_general(wsmallt_ref[...], hb, (((1,), (1,)), ((), ())), preferred_element_type=F32)
    b_scr[...] = _sigmoid(ps)
    g_col = -jnp.exp(alc_ref[...]) * _softplus(ps + dtc_ref[...])
    g_row = -jnp.exp(alr_ref[...]) * _softplus(pst + dtr_ref[...])
    ri = lax.broadcasted_iota(jnp.int32, (lt, lt), 0)
    ci = lax.broadcasted_iota(jnp.int32, (lt, lt), 1)
    same = (ri // CHUNK) == (ci // CHUNK)
    gc_scr[...] = _dot_exact(jnp.where(same & (ci <= ri), 1.0, 0.0).astype(F32), g_col)
    gt_scr[...] = _dot_exact(jnp.where(same, 1.0, 0.0).astype(F32), g_col)
    rj = lax.broadcasted_iota(jnp.int32, (CHUNK, CHUNK), 0)
    cj = lax.broadcasted_iota(jnp.int32, (CHUNK, CHUNK), 1)
    triu = jnp.where(rj <= cj, 1.0, 0.0).astype(F32)
    for c in range(n_chunks):
        gcr_scr[c] = _dot_exact(g_row[:, c * CHUNK:(c + 1) * CHUNK], triu)

    causal = rj >= cj
    strict = rj > cj
    eye = jnp.where(rj == cj, 1.0, 0.0).astype(F32)

    def chunk_body(c, carry):
        r0 = pl.multiple_of(c * CHUNK, CHUNK)
        rows = pl.ds(r0, CHUNK)
        beta_blk = b_scr[rows, :]
        gc_blk = gc_scr[rows, :]
        gt_blk = gt_scr[rows, :]
        gcr_blk = gcr_scr[c]
        for hd in range(N_HEADS):
            sl = slice(hd * HEAD, (hd + 1) * HEAD)
            qh = q_scr[rows, sl]
            kh = k_scr[rows, sl]
            vh = v_scr[rows, sl]
            bcol = _lane_bcast(beta_blk, hd, HEAD)
            gcol = _lane_bcast(gc_blk, N_HEADS + hd, HEAD)
            gtot = _lane_bcast(gt_blk, N_HEADS + hd, HEAD)
            grow = gcr_blk[N_HEADS + hd:N_HEADS + hd + 1, :]
            decay = jnp.exp(jnp.where(causal, gcol[:, 0:CHUNK] - grow, -jnp.inf))
            egc = jnp.exp(gcol)
            kb = kh * bcol
            kk = _dot_nt(kb, kh)
            xm = -jnp.where(strict, kk * decay, 0.0)
            inv = eye + xm
            pw = xm
            n = 2
            while n < CHUNK:
                pw = _dot_exact(pw, pw)
                inv = inv + _dot_exact(pw, inv)
                n *= 2
            rhs = jnp.concatenate([vh * bcol, kb * egc], axis=-1)
            sol = _dot_exact(inv, rhs)
            uu = sol[:, 0:HEAD]
            ww = sol[:, HEAD:2 * HEAD]
            aqk = _dot_nt(qh, kh) * decay
            s_old = s_scr[hd]
            vn = uu - _dot(ww, s_old)
            o_scr[rows, sl] = _dot(qh * egc, s_old) + _dot(aqk, vn)
            egt = jnp.exp(gtot)
            kd = kh * jnp.exp(gtot - gcol)
            s_scr[hd] = s_old * jnp.concatenate([egt] * (HEAD // CHUNK), axis=0) + _dot_tn(kd, vn)
        return carry

    lax.fori_loop(0, n_chunks, chunk_body, 0)

    pr = jnp.dot(hb, wmain_ref[:, 4 * d + dqkv:7 * d + dqkv], preferred_element_type=F32)
    on = _head_rms_gate(o_scr[...], onw_ref[...], pr[:, 0:d])
    yb = jnp.dot(on.astype(BF16), wob_ref[...], preferred_element_type=F32)
    merged = _sigmoid(pr[:, d:2 * d]) * ya + _sigmoid(pr[:, 2 * d:3 * d]) * yb
    post = _rms(jnp.dot(merged.astype(BF16), wo_ref[...], preferred_element_type=F32), npost_ref[...])
    y_ref[0] = x + gate_ref[0] * post

    @pl.when(t == pl.num_programs(1) - 1)
    def _():
        s_ref[0] = s_scr[...]


def _prompt(x, shift, scale, gate, consts):
    bsz, seq, d = x.shape
    dqkv = 3 * d
    lt = PROMPT_TILE
    pad = V7X_SUBLANES
    vm = pl.BlockSpec(memory_space=pltpu.VMEM)
    per_b = pl.BlockSpec((1, 1, d), lambda b, t: (b, 0, 0))
    return pl.pallas_call(
        _prompt_kernel,
        grid=(bsz, seq // lt),
        in_specs=[pl.BlockSpec((1, lt, d), lambda b, t: (b, t, 0)), per_b, per_b, per_b] + [vm] * len(consts),
        out_specs=[pl.BlockSpec((1, lt, d), lambda b, t: (b, t, 0)),
                   pl.BlockSpec((1, CONV_A_W - 1, d), lambda b, t: (b, 0, 0)),
                   pl.BlockSpec((1, CONV_B_W - 1, dqkv), lambda b, t: (b, 0, 0)),
                   pl.BlockSpec((1, N_HEADS, HEAD, HEAD), lambda b, t: (b, 0, 0, 0))],
        out_shape=[jax.ShapeDtypeStruct((bsz, seq, d), F32),
                   jax.ShapeDtypeStruct((bsz, CONV_A_W - 1, d), F32),
                   jax.ShapeDtypeStruct((bsz, CONV_B_W - 1, dqkv), F32),
                   jax.ShapeDtypeStruct((bsz, N_HEADS, HEAD, HEAD), F32)],
        scratch_shapes=[pltpu.VMEM((pad + lt, d), F32),
                        pltpu.VMEM((pad + lt, dqkv), F32),
                        pltpu.VMEM((N_HEADS, HEAD, HEAD), F32),
                        pltpu.VMEM((lt, d), F32), pltpu.VMEM((lt, d), F32),
                        pltpu.VMEM((lt, d), F32), pltpu.VMEM((lt, d), F32),
                        pltpu.VMEM((lt, V7X_LANES), F32), pltpu.VMEM((lt, V7X_LANES), F32),
                        pltpu.VMEM((lt, V7X_LANES), F32),
                        pltpu.VMEM((lt // CHUNK, 2 * N_HEADS, CHUNK), F32)],
        compiler_params=pltpu.CompilerParams(dimension_semantics=("arbitrary", "arbitrary"),
                                             vmem_limit_bytes=V7X_VMEM_LIMIT_BYTES),
        name="prompt",
    )(x, shift, scale, gate, *consts)


def _sample_front_kernel(x_ref, shift_ref, scale_ref, sa_ref, sb_ref, npre_ref, wmain_ref, wsmall_ref,
                         caw_ref, cbw_ref, alc_ref, dtc_ref, woa_ref,
                         q_ref, k_ref, v_ref, be_ref, eg_ref, ya_ref, rest_ref, nsa_ref, nsb_ref):
    d = x_ref.shape[1]
    dqkv = 3 * d
    n = x_ref.shape[0]
    x = x_ref[...]
    h = _rms(x, npre_ref[...]) * (1.0 + scale_ref[...]) + shift_ref[...]
    hb = h.astype(BF16)

    pa = jnp.dot(hb, wmain_ref[:, 0:4 * d], preferred_element_type=F32)
    u = pa[:, d:2 * d] * pa[:, 0:d]
    conv = u * caw_ref[CONV_A_W - 1:CONV_A_W, :]
    for j in range(CONV_A_W - 1):
        conv = conv + sa_ref[:, j * d:(j + 1) * d] * caw_ref[j:j + 1, :]
    ya_ref[...] = jnp.dot(((pa[:, 2 * d:3 * d] * conv) * _silu(pa[:, 3 * d:4 * d])).astype(BF16), woa_ref[...],
                          preferred_element_type=F32)
    nsa_ref[:, 0:(CONV_A_W - 2) * d] = sa_ref[:, d:(CONV_A_W - 1) * d]
    nsa_ref[:, (CONV_A_W - 2) * d:(CONV_A_W - 1) * d] = u

    pq = jnp.dot(hb, wmain_ref[:, 4 * d:4 * d + dqkv], preferred_element_type=F32)
    cq = pq * cbw_ref[CONV_B_W - 1:CONV_B_W, :]
    for j in range(CONV_B_W - 1):
        cq = cq + sb_ref[:, j * dqkv:(j + 1) * dqkv] * cbw_ref[j:j + 1, :]
    cq = _silu(cq)
    nsb_ref[:, 0:(CONV_B_W - 2) * dqkv] = sb_ref[:, dqkv:(CONV_B_W - 1) * dqkv]
    nsb_ref[:, (CONV_B_W - 2) * dqkv:(CONV_B_W - 1) * dqkv] = pq
    q_ref[...] = _head_l2norm(cq[:, 0:d], HEAD ** -0.5)
    k_ref[...] = _head_l2norm(cq[:, d:2 * d], 1.0)
    v_ref[...] = cq[:, 2 * d:3 * d]

    rest_ref[...] = jnp.dot(hb, wmain_ref[:, 4 * d + dqkv:7 * d + dqkv], preferred_element_type=F32)

    ps = jnp.dot(hb, wsmall_ref[...], preferred_element_type=F32)
    beta = _sigmoid(ps)
    eg = jnp.exp(-jnp.exp(alc_ref[...]) * _softplus(ps + dtc_ref[...]))
    be_ref[...] = jnp.concatenate([_lane_bcast(beta, hd, HEAD) for hd in range(N_HEADS)], axis=-1)
    eg_ref[...] = jnp.concatenate([_lane_bcast(eg, N_HEADS + hd, HEAD) for hd in range(N_HEADS)], axis=-1)


def _sample_state_kernel(s_ref, q_ref, k_ref, v_ref, be_ref, eg_ref, so_ref, o_ref):
    bb = q_ref.shape[0]
    row = lax.broadcasted_iota(jnp.int32, (bb, HEAD), 0)
    erow = lax.broadcasted_iota(jnp.int32, (bb, bb * HEAD), 0)
    elane = lax.broadcasted_iota(jnp.int32, (bb, bb * HEAD), 1)
    diag = (elane // HEAD) == erow
    zeros_pad = jnp.zeros((2 * V7X_SUBLANES - bb, HEAD), F32)
    for hd in range(N_HEADS):
        sl = slice(hd * HEAD, (hd + 1) * HEAD)
        qh = q_ref[:, sl]
        kh = k_ref[:, sl]
        vh = v_ref[:, sl]
        eg = eg_ref[:, sl]
        kq = jnp.concatenate([kh, qh], axis=0).astype(BF16)
        ks = jnp.zeros((bb, HEAD), F32)
        qs = jnp.zeros((bb, HEAD), F32)
        for tk in range(bb):
            r = jnp.dot(kq, s_ref[tk, hd].astype(BF16), preferred_element_type=F32)
            ks = jnp.where(row == tk, r[0:bb], ks)
            qs = jnp.where(row == tk, r[bb:2 * bb], qs)
        vn = be_ref[:, sl] * (vh - eg * ks)
        o_ref[:, sl] = eg * qs + jnp.sum(qh * kh, axis=-1, keepdims=True) * vn
        vexp = jnp.where(diag, jnp.concatenate([vn] * bb, axis=-1), 0.0)
        outer = _dot_tn(jnp.concatenate([kh, zeros_pad], axis=0),
                        jnp.concatenate([vexp, jnp.zeros((2 * V7X_SUBLANES - bb, bb * HEAD), F32)], axis=0))
        for tk in range(bb):
            so_ref[tk, hd] = (s_ref[tk, hd] * jnp.broadcast_to(eg[tk:tk + 1, :], (HEAD, HEAD))
                              + outer[:, tk * HEAD:(tk + 1) * HEAD])


def _sample_back_kernel(x_ref, gate_ref, o_ref, ya_ref, rest_ref, onw_ref, wob_ref, wo_ref, npost_ref, y_ref):
    d = x_ref.shape[1]
    on = _head_rms_gate(o_ref[...], onw_ref[...], rest_ref[:, 0:d])
    yb = jnp.dot(on.astype(BF16), wob_ref[...], preferred_element_type=F32)
    merged = _sigmoid(rest_ref[:, d:2 * d]) * ya_ref[...] + _sigmoid(rest_ref[:, 2 * d:3 * d]) * yb
    post = _rms(jnp.dot(merged.astype(BF16), wo_ref[...], preferred_element_type=F32), npost_ref[...])
    y_ref[...] = x_ref[...] + gate_ref[...] * post


def _sample(x, shift, scale, gate, sa, sb, s0, w):
    n, d = x.shape
    dqkv = 3 * d
    vm = pl.BlockSpec(memory_space=pltpu.VMEM)
    params = pltpu.CompilerParams(vmem_limit_bytes=V7X_VMEM_LIMIT_BYTES)
    row = jax.ShapeDtypeStruct((n, d), F32)
    front_in = (x, shift, scale, sa, sb, w["npre"], w["wmain"], w["wsmall"], w["caw"], w["cbw"],
                w["alc"], w["dtc"], w["woa"])
    q, k, v, be, eg, ya, rest, nsa, nsb = pl.pallas_call(
        _sample_front_kernel,
        in_specs=[vm] * len(front_in),
        out_specs=[vm] * 9,
        out_shape=[row, row, row, row, row, row, jax.ShapeDtypeStruct((n, 3 * d), F32),
                   jax.ShapeDtypeStruct(sa.shape, F32), jax.ShapeDtypeStruct(sb.shape, F32)],
        compiler_params=params,
        name="sample_front",
    )(*front_in)

    bb = SAMPLE_BLOCK
    tok = pl.BlockSpec((bb, d), lambda i: (i, 0))
    st = pl.BlockSpec((bb, N_HEADS, HEAD, HEAD), lambda i: (i, 0, 0, 0))
    s_new, o = pl.pallas_call(
        _sample_state_kernel,
        grid=(n // bb,),
        in_specs=[st, tok, tok, tok, tok, tok],
        out_specs=[st, tok],
        out_shape=[jax.ShapeDtypeStruct(s0.shape, F32), row],
        compiler_params=pltpu.CompilerParams(dimension_semantics=("arbitrary",),
                                             vmem_limit_bytes=V7X_VMEM_LIMIT_BYTES),
        name="sample_state",
    )(s0, q, k, v, be, eg)

    back_in = (x, gate, o, ya, rest, w["onw"], w["wob"], w["wo"], w["npost"])
    y = pl.pallas_call(
        _sample_back_kernel,
        in_specs=[vm] * len(back_in),
        out_specs=vm,
        out_shape=row,
        compiler_params=params,
        name="sample_back",
    )(*back_in)
    return y, nsa, nsb, s_new


def kernel(x_prompt, x_sample, c_prompt, c_sample, state_conv_a, state_conv_qkv, state_delta, ada_w, ada_b,
           norm_pre, w_in, conv_a_w, conv_b_w, a_log, dt_bias, onorm_w, w_out_a, w_out_b, w_o, norm_post):
    depth = w_in.shape[0]
    assert depth == 1, "single-layer trunk"
    bp, seq, d = x_prompt.shape
    ns = x_sample.shape[0]
    dqkv = 3 * d
    assert d == N_HEADS * HEAD and seq % PROMPT_TILE == 0 and PROMPT_TILE % CHUNK == 0 and HEAD % CHUNK == 0
    assert ns % SAMPLE_BLOCK == 0 and x_sample.shape[1] == 1

    off_small = 4 * d + dqkv + d
    w0 = w_in[0]
    wmain = jnp.concatenate([w0[:, :off_small], w0[:, off_small + 2 * N_HEADS:]], axis=1).astype(BF16)
    wsmall_f = jnp.pad(w0[:, off_small:off_small + 2 * N_HEADS], ((0, 0), (0, V7X_LANES - 2 * N_HEADS)))
    wsmall = wsmall_f.astype(BF16)
    wsmallt = w0[:, off_small:off_small + 2 * N_HEADS].T.astype(BF16)
    zeros_h = jnp.zeros((N_HEADS,), F32)
    lane_pad = jnp.zeros((V7X_LANES - 2 * N_HEADS,), F32)
    w = {
        "npre": norm_pre, "npost": norm_post, "onw": onorm_w,
        "wmain": wmain, "wsmall": wsmall, "wsmallt": wsmallt,
        "caw": conv_a_w[0], "cbw": conv_b_w[0],
        "alc": jnp.concatenate([zeros_h, a_log[0], lane_pad])[None, :],
        "dtc": jnp.concatenate([zeros_h, dt_bias[0], lane_pad])[None, :],
        "alr": jnp.concatenate([zeros_h, a_log[0]])[:, None],
        "dtr": jnp.concatenate([zeros_h, dt_bias[0]])[:, None],
        "woa": w_out_a[0].astype(BF16), "wob": w_out_b[0].astype(BF16), "wo": w_o[0].astype(BF16),
    }

    mod = _adaln(jnp.concatenate([c_prompt, c_sample], axis=0), ada_w[0], ada_b)
    shift, scale, gate = mod[:, 0:d], mod[:, d:2 * d], mod[:, 2 * d:3 * d]

    consts = (w["npre"], w["wmain"], w["wsmall"], w["wsmallt"], w["caw"], w["cbw"], w["alc"], w["dtc"],
              w["alr"], w["dtr"], w["onw"], w["woa"], w["wob"], w["wo"], w["npost"])
    yp, pca, pcb, pds = _prompt(x_prompt, shift[:bp, None, :], scale[:bp, None, :], gate[:bp, None, :], consts)

    ys, nsa, nsb, sds = _sample(
        x_sample[:, 0, :], shift[bp:], scale[bp:], gate[bp:],
        state_conv_a[0].reshape(ns, (CONV_A_W - 1) * d), state_conv_qkv[0].reshape(ns, (CONV_B_W - 1) * dqkv),
        state_delta[0], w)

    return (yp, ys[:, None, :], pca[None], pcb[None], pds[None].astype(state_delta.dtype),
            nsa.reshape(1, ns, CONV_A_W - 1, d), nsb.reshape(1, ns, CONV_B_W - 1, dqkv),
            sds[None].astype(state_delta.dtype))
```

```python
import functools

import jax
import jax.numpy as jnp
from jax import lax
from jax.experimental import pallas as pl
from jax.experimental.pallas import tpu as pltpu

F32 = jnp.float32
BF16 = jnp.bfloat16

EPS = 1e-6
N_HEADS = 8
HEAD = 128
CONV_A_W = 3
CONV_B_W = 4

V7X_LANES = 128
V7X_SUBLANES = 8
V7X_VMEM_LIMIT_BYTES = 60 * 1024 * 1024

PROMPT_TILE = 256
CHUNK = 64
SAMPLE_BLOCK = 8

HI = lax.Precision.HIGHEST


def _sigmoid(x):
    return 1.0 / (1.0 + jnp.exp(-x))


def _silu(x):
    return x * _sigmoid(x)


def _softplus(x):
    return jnp.maximum(x, 0.0) + jnp.log1p(jnp.exp(-jnp.abs(x)))


def _dot(a, b):
    return jnp.dot(a.astype(BF16), b.astype(BF16), preferred_element_type=F32)


def _dot_nt(a, b):
    return lax.dot_general(a.astype(BF16), b.astype(BF16), (((1,), (1,)), ((), ())),
                           preferred_element_type=F32)


def _dot_tn(a, b):
    return lax.dot_general(a.astype(BF16), b.astype(BF16), (((0,), (0,)), ((), ())),
                           preferred_element_type=F32)


def _dot_exact(a, b):
    return jnp.dot(a, b, precision=HI, preferred_element_type=F32)


def _split(a):
    hi = a.astype(BF16)
    return hi, (a - hi.astype(F32)).astype(BF16)


def _solve_dot(a, b):
    ah, al = _split(a)
    bh, bl = _split(b)
    return (jnp.dot(ah, bh, preferred_element_type=F32) + jnp.dot(ah, bl, preferred_element_type=F32)
            + jnp.dot(al, bh, preferred_element_type=F32))


def _rms(x, w):
    return x * lax.rsqrt(jnp.mean(x * x, axis=-1, keepdims=True) + EPS) * w


def _head_l2norm(x, scale):
    outs = []
    for h in range(N_HEADS):
        xh = x[:, h * HEAD:(h + 1) * HEAD]
        outs.append(xh * (lax.rsqrt(jnp.sum(xh * xh, axis=-1, keepdims=True) + EPS) * scale))
    return jnp.concatenate(outs, axis=-1)


def _head_rms_gate(o, w, z):
    outs = []
    for h in range(N_HEADS):
        sl = slice(h * HEAD, (h + 1) * HEAD)
        oh = o[:, sl]
        outs.append(oh * lax.rsqrt(jnp.mean(oh * oh, axis=-1, keepdims=True) + EPS) * w * _silu(z[:, sl]))
    return jnp.concatenate(outs, axis=-1)


def _lane_bcast(x, col, width):
    return jnp.broadcast_to(x[:, col:col + 1], (x.shape[0], width))


def _adaln_kernel(c_ref, w_ref, b_ref, o_ref):
    o_ref[...] = _dot(_silu(c_ref[...]), w_ref[...]) + b_ref[...]


def _adaln(c_all, ada_w, ada_b):
    n, d = c_all.shape
    cols = ada_w.shape[1]
    return pl.pallas_call(
        _adaln_kernel,
        grid=(cols // d,),
        in_specs=[pl.BlockSpec((n, d), lambda j: (0, 0)),
                  pl.BlockSpec((d, d), lambda j: (0, j)),
                  pl.BlockSpec((1, d), lambda j: (0, j))],
        out_specs=pl.BlockSpec((n, d), lambda j: (0, j)),
        out_shape=jax.ShapeDtypeStruct((n, cols), F32),
        name="adaln",
    )(c_all, ada_w, ada_b)


def _prompt_kernel(x_ref, shift_ref, scale_ref, gate_ref, npre_ref, wmain_ref, wsmall_ref, wsmallt_ref,
                   caw_ref, cbw_ref, alc_ref, dtc_ref, alr_ref, dtr_ref, onw_ref, woa_ref, wob_ref, wo_ref,
                   npost_ref,
                   y_ref, ca_ref, cb_ref, s_ref,
                   abuf, qkvbuf, s_scr, q_scr, k_scr, v_scr, o_scr, b_scr, gc_scr, gt_scr, gcr_scr,
                   pw_scr, inv_scr, aqk_scr, rhs_scr, u_scr, wq_scr, kd_scr, egt_scr):
    lt = x_ref.shape[1]
    d = x_ref.shape[2]
    dqkv = qkvbuf.shape[1]
    n_chunks = lt // CHUNK
    pad = V7X_SUBLANES
    t = pl.program_id(1)

    @pl.when(t == 0)
    def _():
        abuf[0:pad, :] = jnp.zeros((pad, d), F32)
        qkvbuf[0:pad, :] = jnp.zeros((pad, dqkv), F32)
        s_scr[...] = jnp.zeros(s_scr.shape, F32)

    x = x_ref[0]
    h = _rms(x, npre_ref[...]) * (1.0 + scale_ref[0]) + shift_ref[0]
    hb = h.astype(BF16)

    pa = jnp.dot(hb, wmain_ref[:, 0:4 * d], preferred_element_type=F32)
    u = pa[:, d:2 * d] * pa[:, 0:d]
    abuf[pad:pad + lt, :] = u
    conv = u * caw_ref[CONV_A_W - 1:CONV_A_W, :]
    for j in range(CONV_A_W - 1):
        conv = conv + abuf[pad - (CONV_A_W - 1) + j:pad - (CONV_A_W - 1) + j + lt, :] * caw_ref[j:j + 1, :]
    ya = jnp.dot(((pa[:, 2 * d:3 * d] * conv) * _silu(pa[:, 3 * d:4 * d])).astype(BF16), woa_ref[...],
                 preferred_element_type=F32)
    ca_ref[0] = abuf[pad + lt - (CONV_A_W - 1):pad + lt, :]
    abuf[0:pad, :] = abuf[lt:lt + pad, :]

    pq = jnp.dot(hb, wmain_ref[:, 4 * d:4 * d + dqkv], preferred_element_type=F32)
    qkvbuf[pad:pad + lt, :] = pq
    cq = pq * cbw_ref[CONV_B_W - 1:CONV_B_W, :]
    for j in range(CONV_B_W - 1):
        cq = cq + qkvbuf[pad - (CONV_B_W - 1) + j:pad - (CONV_B_W - 1) + j + lt, :] * cbw_ref[j:j + 1, :]
    cq = _silu(cq)
    cb_ref[0] = qkvbuf[pad + lt - (CONV_B_W - 1):pad + lt, :]
    qkvbuf[0:pad, :] = qkvbuf[lt:lt + pad, :]
    q_scr[...] = _head_l2norm(cq[:, 0:d], HEAD ** -0.5)
    k_scr[...] = _head_l2norm(cq[:, d:2 * d], 1.0)
    v_scr[...] = cq[:, 2 * d:3 * d]

    ps = jnp.dot(hb, wsmall_ref[...], preferred_element_type=F32)
    pst = lax.dot_general(wsmallt_ref[...], hb, (((1,), (1,)), ((), ())), preferred_element_type=F32)
    b_scr[...] = _sigmoid(ps)
    g_col = -jnp.exp(alc_ref[...]) * _softplus(ps + dtc_ref[...])
    g_row = -jnp.exp(alr_ref[...]) * _softplus(pst + dtr_ref[...])
    ri = lax.broadcasted_iota(jnp.int32, (lt, lt), 0)
    ci = lax.broadcasted_iota(jnp.int32, (lt, lt), 1)
    same = (ri // CHUNK) == (ci // CHUNK)
    gc_scr[...] = _dot_exact(jnp.where(same & (ci <= ri), 1.0, 0.0).astype(F32), g_col)
    gt_scr[...] = _dot_exact(jnp.where(same, 1.0, 0.0).astype(F32), g_col)
    rj = lax.broadcasted_iota(jnp.int32, (CHUNK, CHUNK), 0)
    cj = lax.broadcasted_iota(jnp.int32, (CHUNK, CHUNK), 1)
    triu = jnp.where(rj <= cj, 1.0, 0.0).astype(F32)
    for c in range(n_chunks):
        gcr_scr[c] = _dot_exact(g_row[:, c * CHUNK:(c + 1) * CHUNK], triu)

    causal = rj >= cj
    strict = rj > cj
    eye = jnp.where(rj == cj, 1.0, 0.0).astype(F32)

    nb = n_chunks * N_HEADS

    for c in range(n_chunks):
        rows = slice(c * CHUNK, (c + 1) * CHUNK)
        beta_blk = b_scr[rows, :]
        gc_blk = gc_scr[rows, :]
        gt_blk = gt_scr[rows, :]
        gcr_blk = gcr_scr[c]
        for hd in range(N_HEADS):
            n = c * N_HEADS + hd
            sl = slice(hd * HEAD, (hd + 1) * HEAD)
            qh = q_scr[rows, sl]
            kh = k_scr[rows, sl]
            vh = v_scr[rows, sl]
            bcol = _lane_bcast(beta_blk, hd, HEAD)
            gcol = _lane_bcast(gc_blk, N_HEADS + hd, HEAD)
            gtot = _lane_bcast(gt_blk, N_HEADS + hd, HEAD)
            grow = gcr_blk[N_HEADS + hd:N_HEADS + hd + 1, :]
            decay = jnp.exp(jnp.where(causal, gcol[:, 0:CHUNK] - grow, -jnp.inf))
            egc = jnp.exp(gcol)
            kb = kh * bcol
            xm = -jnp.where(strict, _dot_nt(kb, kh) * decay, 0.0)
            pw_scr[n] = xm
            inv_scr[n] = eye + xm
            aqk_scr[n] = (_dot_nt(qh, kh) * decay).astype(BF16)
            rhs_scr[n] = jnp.concatenate([vh * bcol, kb * egc], axis=-1)
            wq_scr[n, CHUNK:2 * CHUNK, :] = (qh * egc).astype(BF16)
            kd_scr[n] = (kh * jnp.exp(gtot - gcol)).astype(BF16)
            egt_scr[n] = jnp.exp(gtot[0:V7X_SUBLANES, :])

    for n in range(nb):
        pw = pw_scr[n]
        pw_scr[n] = _solve_dot(pw, pw)
    m = 2
    while m < CHUNK:
        for n in range(nb):
            pw = pw_scr[n]
            inv = inv_scr[n]
            inv_scr[n] = inv + _solve_dot(pw, inv)
            if 2 * m < CHUNK:
                pw_scr[n] = _solve_dot(pw, pw)
        m *= 2
    for n in range(nb):
        sol = _solve_dot(inv_scr[n], rhs_scr[n])
        u_scr[n] = sol[:, 0:HEAD]
        wq_scr[n, 0:CHUNK, :] = sol[:, HEAD:2 * HEAD].astype(BF16)

    for c in range(n_chunks):
        rows = slice(c * CHUNK, (c + 1) * CHUNK)
        for hd in range(N_HEADS):
            n = c * N_HEADS + hd
            sl = slice(hd * HEAD, (hd + 1) * HEAD)
            s_old = s_scr[hd]
            ws = jnp.dot(wq_scr[n], s_old.astype(BF16), preferred_element_type=F32)
            vnb = (u_scr[n] - ws[0:CHUNK]).astype(BF16)
            o_scr[rows, sl] = ws[CHUNK:2 * CHUNK] + jnp.dot(aqk_scr[n], vnb, preferred_element_type=F32)
            s_scr[hd] = (s_old * jnp.concatenate([egt_scr[n]] * (HEAD // V7X_SUBLANES), axis=0)
                         + lax.dot_general(kd_scr[n], vnb, (((0,), (0,)), ((), ())), preferred_element_type=F32))

    pr = jnp.dot(hb, wmain_ref[:, 4 * d + dqkv:7 * d + dqkv], preferred_element_type=F32)
    on = _head_rms_gate(o_scr[...], onw_ref[...], pr[:, 0:d])
    yb = jnp.dot(on.astype(BF16), wob_ref[...], preferred_element_type=F32)
    merged = _sigmoid(pr[:, d:2 * d]) * ya + _sigmoid(pr[:, 2 * d:3 * d]) * yb
    post = _rms(jnp.dot(merged.astype(BF16), wo_ref[...], preferred_element_type=F32), npost_ref[...])
    y_ref[0] = x + gate_ref[0] * post

    @pl.when(t == pl.num_programs(1) - 1)
    def _():
        s_ref[0] = s_scr[...]


def _prompt(x, shift, scale, gate, consts):
    bsz, seq, d = x.shape
    dqkv = 3 * d
    lt = PROMPT_TILE
    pad = V7X_SUBLANES
    nb = (lt // CHUNK) * N_HEADS
    vm = pl.BlockSpec(memory_space=pltpu.VMEM)
    per_b = pl.BlockSpec((1, 1, d), lambda b, t: (b, 0, 0))
    return pl.pallas_call(
        _prompt_kernel,
        grid=(bsz, seq // lt),
        in_specs=[pl.BlockSpec((1, lt, d), lambda b, t: (b, t, 0)), per_b, per_b, per_b] + [vm] * len(consts),
        out_specs=[pl.BlockSpec((1, lt, d), lambda b, t: (b, t, 0)),
                   pl.BlockSpec((1, CONV_A_W - 1, d), lambda b, t: (b, 0, 0)),
                   pl.BlockSpec((1, CONV_B_W - 1, dqkv), lambda b, t: (b, 0, 0)),
                   pl.BlockSpec((1, N_HEADS, HEAD, HEAD), lambda b, t: (b, 0, 0, 0))],
        out_shape=[jax.ShapeDtypeStruct((bsz, seq, d), F32),
                   jax.ShapeDtypeStruct((bsz, CONV_A_W - 1, d), F32),
                   jax.ShapeDtypeStruct((bsz, CONV_B_W - 1, dqkv), F32),
                   jax.ShapeDtypeStruct((bsz, N_HEADS, HEAD, HEAD), F32)],
        scratch_shapes=[pltpu.VMEM((pad + lt, d), F32),
                        pltpu.VMEM((pad + lt, dqkv), F32),
                        pltpu.VMEM((N_HEADS, HEAD, HEAD), F32),
                        pltpu.VMEM((lt, d), F32), pltpu.VMEM((lt, d), F32),
                        pltpu.VMEM((lt, d), F32), pltpu.VMEM((lt, d), F32),
                        pltpu.VMEM((lt, V7X_LANES), F32), pltpu.VMEM((lt, V7X_LANES), F32),
                        pltpu.VMEM((lt, V7X_LANES), F32),
                        pltpu.VMEM((lt // CHUNK, 2 * N_HEADS, CHUNK), F32),
                        pltpu.VMEM((nb, CHUNK, CHUNK), F32), pltpu.VMEM((nb, CHUNK, CHUNK), F32),
                        pltpu.VMEM((nb, CHUNK, CHUNK), BF16), pltpu.VMEM((nb, CHUNK, 2 * HEAD), F32),
                        pltpu.VMEM((nb, CHUNK, HEAD), F32), pltpu.VMEM((nb, 2 * CHUNK, HEAD), BF16),
                        pltpu.VMEM((nb, CHUNK, HEAD), BF16), pltpu.VMEM((nb, V7X_SUBLANES, HEAD), F32)],
        compiler_params=pltpu.CompilerParams(dimension_semantics=("arbitrary", "arbitrary"),
                                             vmem_limit_bytes=V7X_VMEM_LIMIT_BYTES),
        name="prompt",
    )(x, shift, scale, gate, *consts)


def _sample_front_kernel(x_ref, shift_ref, scale_ref, sa_ref, sb_ref, npre_ref, wmain_ref, wsmall_ref,
                         caw_ref, cbw_ref, alc_ref, dtc_ref, woa_ref,
                         q_ref, k_ref, v_ref, be_ref, eg_ref, ya_ref, rest_ref, nsa_ref, nsb_ref):
    d = x_ref.shape[1]
    dqkv = 3 * d
    n = x_ref.shape[0]
    x = x_ref[...]
    h = _rms(x, npre_ref[...]) * (1.0 + scale_ref[...]) + shift_ref[...]
    hb = h.astype(BF16)

    pa = jnp.dot(hb, wmain_ref[:, 0:4 * d], preferred_element_type=F32)
    u = pa[:, d:2 * d] * pa[:, 0:d]
    conv = u * caw_ref[CONV_A_W - 1:CONV_A_W, :]
    for j in range(CONV_A_W - 1):
        conv = conv + sa_ref[:, j * d:(j + 1) * d] * caw_ref[j:j + 1, :]
    ya_ref[...] = jnp.dot(((pa[:, 2 * d:3 * d] * conv) * _silu(pa[:, 3 * d:4 * d])).astype(BF16), woa_ref[...],
                          preferred_element_type=F32)
    nsa_ref[:, 0:(CONV_A_W - 2) * d] = sa_ref[:, d:(CONV_A_W - 1) * d]
    nsa_ref[:, (CONV_A_W - 2) * d:(CONV_A_W - 1) * d] = u

    pq = jnp.dot(hb, wmain_ref[:, 4 * d:4 * d + dqkv], preferred_element_type=F32)
    cq = pq * cbw_ref[CONV_B_W - 1:CONV_B_W, :]
    for j in range(CONV_B_W - 1):
        cq = cq + sb_ref[:, j * dqkv:(j + 1) * dqkv] * cbw_ref[j:j + 1, :]
    cq = _silu(cq)
    nsb_ref[:, 0:(CONV_B_W - 2) * dqkv] = sb_ref[:, dqkv:(CONV_B_W - 1) * dqkv]
    nsb_ref[:, (CONV_B_W - 2) * dqkv:(CONV_B_W - 1) * dqkv] = pq
    q_ref[...] = _head_l2norm(cq[:, 0:d], HEAD ** -0.5)
    k_ref[...] = _head_l2norm(cq[:, d:2 * d], 1.0)
    v_ref[...] = cq[:, 2 * d:3 * d]

    rest_ref[...] = jnp.dot(hb, wmain_ref[:, 4 * d + dqkv:7 * d + dqkv], preferred_element_type=F32)

    ps = jnp.dot(hb, wsmall_ref[...], preferred_element_type=F32)
    beta = _sigmoid(ps)
    eg = jnp.exp(-jnp.exp(alc_ref[...]) * _softplus(ps + dtc_ref[...]))
    be_ref[...] = jnp.concatenate([_lane_bcast(beta, hd, HEAD) for hd in range(N_HEADS)], axis=-1)
    eg_ref[...] = jnp.concatenate([_lane_bcast(eg, N_HEADS + hd, HEAD) for hd in range(N_HEADS)], axis=-1)


def _sample_state_kernel(s_ref, q_ref, k_ref, v_ref, be_ref, eg_ref, so_ref, o_ref):
    bb = q_ref.shape[0]
    row = lax.broadcasted_iota(jnp.int32, (bb, HEAD), 0)
    erow = lax.broadcasted_iota(jnp.int32, (bb, bb * HEAD), 0)
    elane = lax.broadcasted_iota(jnp.int32, (bb, bb * HEAD), 1)
    diag = (elane // HEAD) == erow
    zeros_pad = jnp.zeros((2 * V7X_SUBLANES - bb, HEAD), F32)
    for hd in range(N_HEADS):
        sl = slice(hd * HEAD, (hd + 1) * HEAD)
        qh = q_ref[:, sl]
        kh = k_ref[:, sl]
        vh = v_ref[:, sl]
        eg = eg_ref[:, sl]
        kq = jnp.concatenate([kh, qh], axis=0).astype(BF16)
        ks = jnp.zeros((bb, HEAD), F32)
        qs = jnp.zeros((bb, HEAD), F32)
        for tk in range(bb):
            r = jnp.dot(kq, s_ref[tk, hd].astype(BF16), preferred_element_type=F32)
            ks = jnp.where(row == tk, r[0:bb], ks)
            qs = jnp.where(row == tk, r[bb:2 * bb], qs)
        vn = be_ref[:, sl] * (vh - eg * ks)
        o_ref[:, sl] = eg * qs + jnp.sum(qh * kh, axis=-1, keepdims=True) * vn
        vexp = jnp.where(diag, jnp.concatenate([vn] * bb, axis=-1), 0.0)
        outer = _dot_tn(jnp.concatenate([kh, zeros_pad], axis=0),
                        jnp.concatenate([vexp, jnp.zeros((2 * V7X_SUBLANES - bb, bb * HEAD), F32)], axis=0))
        for tk in range(bb):
            so_ref[tk, hd] = (s_ref[tk, hd] * jnp.broadcast_to(eg[tk:tk + 1, :], (HEAD, HEAD))
                              + outer[:, tk * HEAD:(tk + 1) * HEAD])


def _sample_back_kernel(x_ref, gate_ref, o_ref, ya_ref, rest_ref, onw_ref, wob_ref, wo_ref, npost_ref, y_ref):
    d = x_ref.shape[1]
    on = _head_rms_gate(o_ref[...], onw_ref[...], rest_ref[:, 0:d])
    yb = jnp.dot(on.astype(BF16), wob_ref[...], preferred_element_type=F32)
    merged = _sigmoid(rest_ref[:, d:2 * d]) * ya_ref[...] + _sigmoid(rest_ref[:, 2 * d:3 * d]) * yb
    post = _rms(jnp.dot(merged.astype(BF16), wo_ref[...], preferred_element_type=F32), npost_ref[...])
    y_ref[...] = x_ref[...] + gate_ref[...] * post


def _sample(x, shift, scale, gate, sa, sb, s0, w):
    n, d = x.shape
    dqkv = 3 * d
    vm = pl.BlockSpec(memory_space=pltpu.VMEM)
    params = pltpu.CompilerParams(vmem_limit_bytes=V7X_VMEM_LIMIT_BYTES)
    row = jax.ShapeDtypeStruct((n, d), F32)
    front_in = (x, shift, scale, sa, sb, w["npre"], w["wmain"], w["wsmall"], w["caw"], w["cbw"],
                w["alc"], w["dtc"], w["woa"])
    q, k, v, be, eg, ya, rest, nsa, nsb = pl.pallas_call(
        _sample_front_kernel,
        in_specs=[vm] * len(front_in),
        out_specs=[vm] * 9,
        out_shape=[row, row, row, row, row, row, jax.ShapeDtypeStruct((n, 3 * d), F32),
                   jax.ShapeDtypeStruct(sa.shape, F32), jax.ShapeDtypeStruct(sb.shape, F32)],
        compiler_params=params,
        name="sample_front",
    )(*front_in)

    bb = SAMPLE_BLOCK
    tok = pl.BlockSpec((bb, d), lambda i: (i, 0))
    st = pl.BlockSpec((bb, N_HEADS, HEAD, HEAD), lambda i: (i, 0, 0, 0))
    s_new, o = pl.pallas_call(
        _sample_state_kernel,
        grid=(n // bb,),
        in_specs=[st, tok, tok, tok, tok, tok],
        out_specs=[st, tok],
        out_shape=[jax.ShapeDtypeStruct(s0.shape, F32), row],
        compiler_params=pltpu.CompilerParams(dimension_semantics=("arbitrary",),
                                             vmem_limit_bytes=V7X_VMEM_LIMIT_BYTES),
        name="sample_state",
    )(s0, q, k, v, be, eg)

    back_in = (x, gate, o, ya, rest, w["onw"], w["wob"], w["wo"], w["npost"])
    y = pl.pallas_call(
        _sample_back_kernel,
        in_specs=[vm] * len(back_in),
        out_specs=vm,
        out_shape=row,
        compiler_params=params,
        name="sample_back",
    )(*back_in)
    return y, nsa, nsb, s_new


def kernel(x_prompt, x_sample, c_prompt, c_sample, state_conv_a, state_conv_qkv, state_delta, ada_w, ada_b,
           norm_pre, w_in, conv_a_w, conv_b_w, a_log, dt_bias, onorm_w, w_out_a, w_out_b, w_o, norm_post):
    depth = w_in.shape[0]
    assert depth == 1, "single-layer trunk"
    bp, seq, d = x_prompt.shape
    ns = x_sample.shape[0]
    dqkv = 3 * d
    assert d == N_HEADS * HEAD and seq % PROMPT_TILE == 0 and PROMPT_TILE % CHUNK == 0 and HEAD % CHUNK == 0
    assert ns % SAMPLE_BLOCK == 0 and x_sample.shape[1] == 1

    off_small = 4 * d + dqkv + d
    w0 = w_in[0]
    wmain = jnp.concatenate([w0[:, :off_small], w0[:, off_small + 2 * N_HEADS:]], axis=1).astype(BF16)
    wsmall_f = jnp.pad(w0[:, off_small:off_small + 2 * N_HEADS], ((0, 0), (0, V7X_LANES - 2 * N_HEADS)))
    wsmall = wsmall_f.astype(BF16)
    wsmallt = w0[:, off_small:off_small + 2 * N_HEADS].T.astype(BF16)
    zeros_h = jnp.zeros((N_HEADS,), F32)
    lane_pad = jnp.zeros((V7X_LANES - 2 * N_HEADS,), F32)
    w = {
        "npre": norm_pre, "npost": norm_post, "onw": onorm_w,
        "wmain": wmain, "wsmall": wsmall, "wsmallt": wsmallt,
        "caw": conv_a_w[0], "cbw": conv_b_w[0],
        "alc": jnp.concatenate([zeros_h, a_log[0], lane_pad])[None, :],
        "dtc": jnp.concatenate([zeros_h, dt_bias[0], lane_pad])[None, :],
        "alr": jnp.concatenate([zeros_h, a_log[0]])[:, None],
        "dtr": jnp.concatenate([zeros_h, dt_bias[0]])[:, None],
        "woa": w_out_a[0].astype(BF16), "wob": w_out_b[0].astype(BF16), "wo": w_o[0].astype(BF16),
    }

    mod = _adaln(jnp.concatenate([c_prompt, c_sample], axis=0), ada_w[0], ada_b)
    shift, scale, gate = mod[:, 0:d], mod[:, d:2 * d], mod[:, 2 * d:3 * d]

    consts = (w["npre"], w["wmain"], w["wsmall"], w["wsmallt"], w["caw"], w["cbw"], w["alc"], w["dtc"],
              w["alr"], w["dtr"], w["onw"], w["woa"], w["wob"], w["wo"], w["npost"])
    yp, pca, pcb, pds = _prompt(x_prompt, shift[:bp, None, :], scale[:bp, None, :], gate[:bp, None, :], consts)

    ys, nsa, nsb, sds = _sample(
        x_sample[:, 0, :], shift[bp:], scale[bp:], gate[bp:],
        state_conv_a[0].reshape(ns, (CONV_A_W - 1) * d), state_conv_qkv[0].reshape(ns, (CONV_B_W - 1) * dqkv),
        state_delta[0], w)

    return (yp, ys[:, None, :], pca[None], pcb[None], pds[None].astype(state_delta.dtype),
            nsa.reshape(1, ns, CONV_A_W - 1, d), nsb.reshape(1, ns, CONV_B_W - 1, dqkv),
            sds[None].astype(state_delta.dtype))
```

```python
import functools

import jax
import jax.numpy as jnp
from jax import lax
from jax.experimental import pallas as pl
from jax.experimental.pallas import tpu as pltpu

F32 = jnp.float32
BF16 = jnp.bfloat16

EPS = 1e-6
N_HEADS = 8
HEAD = 128
CONV_A_W = 3
CONV_B_W = 4

V7X_LANES = 128
V7X_SUBLANES = 8
V7X_VMEM_LIMIT_BYTES = 60 * 1024 * 1024

PROMPT_TILE = 256
CHUNK = 64
SAMPLE_BLOCK = 8

HI = lax.Precision.HIGHEST


def _sigmoid(x):
    return 1.0 / (1.0 + jnp.exp(-x))


def _silu(x):
    return x * _sigmoid(x)


def _softplus(x):
    return jnp.maximum(x, 0.0) + jnp.log1p(jnp.exp(-jnp.abs(x)))


def _dot(a, b):
    return jnp.dot(a.astype(BF16), b.astype(BF16), preferred_element_type=F32)


def _dot_nt(a, b):
    return lax.dot_general(a.astype(BF16), b.astype(BF16), (((1,), (1,)), ((), ())),
                           preferred_element_type=F32)


def _dot_tn(a, b):
    return lax.dot_general(a.astype(BF16), b.astype(BF16), (((0,), (0,)), ((), ())),
                           preferred_element_type=F32)


def _dot_exact(a, b):
    return jnp.dot(a, b, precision=HI, preferred_element_type=F32)


def _split(a):
    hi = a.astype(BF16)
    return hi, (a - hi.astype(F32)).astype(BF16)


def _solve_dot(a, b):
    ah, al = _split(a)
    bh, bl = _split(b)
    return (jnp.dot(ah, bh, preferred_element_type=F32) + jnp.dot(ah, bl, preferred_element_type=F32)
            + jnp.dot(al, bh, preferred_element_type=F32))


def _rms(x, w):
    return x * lax.rsqrt(jnp.mean(x * x, axis=-1, keepdims=True) + EPS) * w


def _head_l2norm(x, scale):
    outs = []
    for h in range(N_HEADS):
        xh = x[:, h * HEAD:(h + 1) * HEAD]
        outs.append(xh * (lax.rsqrt(jnp.sum(xh * xh, axis=-1, keepdims=True) + EPS) * scale))
    return jnp.concatenate(outs, axis=-1)


def _head_rms_gate(o, w, z):
    outs = []
    for h in range(N_HEADS):
        sl = slice(h * HEAD, (h + 1) * HEAD)
        oh = o[:, sl]
        outs.append(oh * lax.rsqrt(jnp.mean(oh * oh, axis=-1, keepdims=True) + EPS) * w * _silu(z[:, sl]))
    return jnp.concatenate(outs, axis=-1)


def _lane_bcast(x, col, width):
    return jnp.broadcast_to(x[:, col:col + 1], (x.shape[0], width))


def _adaln_kernel(c_ref, w_ref, b_ref, o_ref):
    o_ref[...] = _dot(_silu(c_ref[...]), w_ref[...]) + b_ref[...]


def _adaln(c_all, ada_w, ada_b):
    n, d = c_all.shape
    cols = ada_w.shape[1]
    return pl.pallas_call(
        _adaln_kernel,
        grid=(cols // d,),
        in_specs=[pl.BlockSpec((n, d), lambda j: (0, 0)),
                  pl.BlockSpec((d, d), lambda j: (0, j)),
                  pl.BlockSpec((1, d), lambda j: (0, j))],
        out_specs=pl.BlockSpec((n, d), lambda j: (0, j)),
        out_shape=jax.ShapeDtypeStruct((n, cols), F32),
        name="adaln",
    )(c_all, ada_w, ada_b)


def _prompt_kernel(x_ref, shift_ref, scale_ref, gate_ref, npre_ref, wmain_ref, wsmall_ref, wsmallt_ref,
                   caw_ref, cbw_ref, alc_ref, dtc_ref, alr_ref, dtr_ref, onw_ref, woa_ref, wob_ref, wo_ref,
                   npost_ref,
                   y_ref, ca_ref, cb_ref, s_ref,
                   abuf, qkvbuf, s_scr, q_scr, k_scr, v_scr, o_scr, b_scr, gc_scr, gt_scr, gcr_scr,
                   r_scr, aqk_scr, rhs_scr, qd_scr, kd_scr, egt_scr, mp_scr, sb_scr, ob_scr, wu_scr):
    lt = x_ref.shape[1]
    d = x_ref.shape[2]
    dqkv = qkvbuf.shape[1]
    n_chunks = lt // CHUNK
    pad = V7X_SUBLANES
    t = pl.program_id(1)

    @pl.when(t == 0)
    def _():
        abuf[0:pad, :] = jnp.zeros((pad, d), F32)
        qkvbuf[0:pad, :] = jnp.zeros((pad, dqkv), F32)
        s_scr[...] = jnp.zeros(s_scr.shape, F32)

    x = x_ref[0]
    h = _rms(x, npre_ref[...]) * (1.0 + scale_ref[0]) + shift_ref[0]
    hb = h.astype(BF16)

    pa = jnp.dot(hb, wmain_ref[:, 0:4 * d], preferred_element_type=F32)
    u = pa[:, d:2 * d] * pa[:, 0:d]
    abuf[pad:pad + lt, :] = u
    conv = u * caw_ref[CONV_A_W - 1:CONV_A_W, :]
    for j in range(CONV_A_W - 1):
        conv = conv + abuf[pad - (CONV_A_W - 1) + j:pad - (CONV_A_W - 1) + j + lt, :] * caw_ref[j:j + 1, :]
    ya = jnp.dot(((pa[:, 2 * d:3 * d] * conv) * _silu(pa[:, 3 * d:4 * d])).astype(BF16), woa_ref[...],
                 preferred_element_type=F32)
    ca_ref[0] = abuf[pad + lt - (CONV_A_W - 1):pad + lt, :]
    abuf[0:pad, :] = abuf[lt:lt + pad, :]

    pq = jnp.dot(hb, wmain_ref[:, 4 * d:4 * d + dqkv], preferred_element_type=F32)
    qkvbuf[pad:pad + lt, :] = pq
    cq = pq * cbw_ref[CONV_B_W - 1:CONV_B_W, :]
    for j in range(CONV_B_W - 1):
        cq = cq + qkvbuf[pad - (CONV_B_W - 1) + j:pad - (CONV_B_W - 1) + j + lt, :] * cbw_ref[j:j + 1, :]
    cq = _silu(cq)
    cb_ref[0] = qkvbuf[pad + lt - (CONV_B_W - 1):pad + lt, :]
    qkvbuf[0:pad, :] = qkvbuf[lt:lt + pad, :]
    q_scr[...] = _head_l2norm(cq[:, 0:d], HEAD ** -0.5)
    k_scr[...] = _head_l2norm(cq[:, d:2 * d], 1.0)
    v_scr[...] = cq[:, 2 * d:3 * d]

    ps = jnp.dot(hb, wsmall_ref[...], preferred_element_type=F32)
    pst = lax.dot_general(wsmallt_ref[...], hb, (((1,), (1,)), ((), ())), preferred_element_type=F32)
    b_scr[...] = _sigmoid(ps)
    g_col = -jnp.exp(alc_ref[...]) * _softplus(ps + dtc_ref[...])
    g_row = -jnp.exp(alr_ref[...]) * _softplus(pst + dtr_ref[...])
    ri = lax.broadcasted_iota(jnp.int32, (lt, lt), 0)
    ci = lax.broadcasted_iota(jnp.int32, (lt, lt), 1)
    same = (ri // CHUNK) == (ci // CHUNK)
    gc_scr[...] = _dot_exact(jnp.where(same & (ci <= ri), 1.0, 0.0).astype(F32), g_col)
    gt_scr[...] = _dot_exact(jnp.where(same, 1.0, 0.0).astype(F32), g_col)
    rj = lax.broadcasted_iota(jnp.int32, (CHUNK, CHUNK), 0)
    cj = lax.broadcasted_iota(jnp.int32, (CHUNK, CHUNK), 1)
    triu = jnp.where(rj <= cj, 1.0, 0.0).astype(F32)
    for c in range(n_chunks):
        gcr_scr[c] = _dot_exact(g_row[:, c * CHUNK:(c + 1) * CHUNK], triu)

    causal = rj >= cj
    strict = rj > cj
    eye = jnp.where(rj == cj, 1.0, 0.0).astype(F32)

    nb = n_chunks * N_HEADS
    lane = lax.broadcasted_iota(jnp.int32, (CHUNK, 2 * CHUNK), 1)
    low_half = lane < CHUNK

    for c in range(n_chunks):
        rows = slice(c * CHUNK, (c + 1) * CHUNK)
        beta_blk = b_scr[rows, :]
        gc_blk = gc_scr[rows, :]
        gt_blk = gt_scr[rows, :]
        gcr_blk = gcr_scr[c]
        for hd in range(N_HEADS):
            n = c * N_HEADS + hd
            sl = slice(hd * HEAD, (hd + 1) * HEAD)
            qh = q_scr[rows, sl]
            kh = k_scr[rows, sl]
            vh = v_scr[rows, sl]
            bcol = _lane_bcast(beta_blk, hd, HEAD)
            gcol = _lane_bcast(gc_blk, N_HEADS + hd, HEAD)
            gtot = _lane_bcast(gt_blk, N_HEADS + hd, HEAD)
            grow = gcr_blk[N_HEADS + hd:N_HEADS + hd + 1, :]
            decay = jnp.exp(jnp.where(causal, gcol[:, 0:CHUNK] - grow, -jnp.inf))
            egc = jnp.exp(gcol)
            kb = kh * bcol
            xm = -jnp.where(strict, _dot_nt(kb, kh) * decay, 0.0)
            r_scr[n] = jnp.concatenate([xm, eye], axis=-1)
            aqk_scr[n] = (_dot_nt(qh, kh) * decay).astype(BF16)
            rhs_scr[n] = jnp.concatenate([vh * bcol, kb * egc], axis=-1)
            qd_scr[n] = qh * egc
            kd_scr[n] = (kh * jnp.exp(gtot - gcol)).astype(BF16)
            egt_scr[n] = jnp.exp(gtot[0:V7X_SUBLANES, :])

    def split_parts(r):
        hi = r.astype(BF16)
        hi32 = hi.astype(F32)
        lo32 = r - hi32
        return hi, hi32, lo32

    m = 1
    while m < CHUNK:
        for n in range(nb):
            r = r_scr[n]
            hi, hi32, lo32 = split_parts(r)
            lhs = jnp.where(low_half, hi32, pltpu.roll(lo32, CHUNK, axis=1)).astype(BF16)
            lo = lo32.astype(BF16)
            prod = jnp.dot(jnp.concatenate([lhs, lhs], axis=1), jnp.concatenate([hi, hi, lo, lo], axis=0),
                           preferred_element_type=F32)
            r_scr[n] = prod + jnp.where(low_half, 0.0, r)
        m *= 2
    for n in range(nb):
        hi, hi32, lo32 = split_parts(r_scr[n])
        lhs = jnp.where(low_half, pltpu.roll(lo32, CHUNK, axis=1), hi32).astype(BF16)
        bh, _, bl32 = split_parts(rhs_scr[n])
        bl = bl32.astype(BF16)
        sol = jnp.dot(jnp.concatenate([lhs, lhs], axis=1), jnp.concatenate([bh, bh, bl, bl], axis=0),
                      preferred_element_type=F32)
        wu_scr[n] = jnp.concatenate([sol[:, HEAD:2 * HEAD], sol[:, 0:HEAD]], axis=1).astype(BF16)
    for n in range(nb):
        wu = wu_scr[n]
        kw = lax.dot_general(kd_scr[n], wu, (((0,), (0,)), ((), ())), preferred_element_type=F32)
        aw = jnp.dot(aqk_scr[n], wu, preferred_element_type=F32)
        mp_scr[n, 0:HEAD, :] = kw[:, 0:HEAD].astype(BF16)
        mp_scr[n, HEAD:HEAD + CHUNK, :] = (qd_scr[n] - aw[:, 0:HEAD]).astype(BF16)
        sb_scr[n] = kw[:, HEAD:2 * HEAD]
        ob_scr[n] = aw[:, HEAD:2 * HEAD]

    for c in range(n_chunks):
        rows = slice(c * CHUNK, (c + 1) * CHUNK)
        for hd in range(N_HEADS):
            n = c * N_HEADS + hd
            sl = slice(hd * HEAD, (hd + 1) * HEAD)
            s_old = s_scr[hd]
            res = jnp.dot(mp_scr[n], s_old.astype(BF16), preferred_element_type=F32)
            o_scr[rows, sl] = res[HEAD:HEAD + CHUNK] + ob_scr[n]
            s_scr[hd] = (s_old * jnp.concatenate([egt_scr[n]] * (HEAD // V7X_SUBLANES), axis=0)
                         - res[0:HEAD] + sb_scr[n])

    pr = jnp.dot(hb, wmain_ref[:, 4 * d + dqkv:7 * d + dqkv], preferred_element_type=F32)
    on = _head_rms_gate(o_scr[...], onw_ref[...], pr[:, 0:d])
    yb = jnp.dot(on.astype(BF16), wob_ref[...], preferred_element_type=F32)
    merged = _sigmoid(pr[:, d:2 * d]) * ya + _sigmoid(pr[:, 2 * d:3 * d]) * yb
    post = _rms(jnp.dot(merged.astype(BF16), wo_ref[...], preferred_element_type=F32), npost_ref[...])
    y_ref[0] = x + gate_ref[0] * post

    @pl.when(t == pl.num_programs(1) - 1)
    def _():
        s_ref[0] = s_scr[...]


def _prompt(x, shift, scale, gate, consts):
    bsz, seq, d = x.shape
    dqkv = 3 * d
    lt = PROMPT_TILE
    pad = V7X_SUBLANES
    nb = (lt // CHUNK) * N_HEADS
    vm = pl.BlockSpec(memory_space=pltpu.VMEM)
    per_b = pl.BlockSpec((1, 1, d), lambda b, t: (b, 0, 0))
    return pl.pallas_call(
        _prompt_kernel,
        grid=(bsz, seq // lt),
        in_specs=[pl.BlockSpec((1, lt, d), lambda b, t: (b, t, 0)), per_b, per_b, per_b] + [vm] * len(consts),
        out_specs=[pl.BlockSpec((1, lt, d), lambda b, t: (b, t, 0)),
                   pl.BlockSpec((1, CONV_A_W - 1, d), lambda b, t: (b, 0, 0)),
                   pl.BlockSpec((1, CONV_B_W - 1, dqkv), lambda b, t: (b, 0, 0)),
                   pl.BlockSpec((1, N_HEADS, HEAD, HEAD), lambda b, t: (b, 0, 0, 0))],
        out_shape=[jax.ShapeDtypeStruct((bsz, seq, d), F32),
                   jax.ShapeDtypeStruct((bsz, CONV_A_W - 1, d), F32),
                   jax.ShapeDtypeStruct((bsz, CONV_B_W - 1, dqkv), F32),
                   jax.ShapeDtypeStruct((bsz, N_HEADS, HEAD, HEAD), F32)],
        scratch_shapes=[pltpu.VMEM((pad + lt, d), F32),
                        pltpu.VMEM((pad + lt, dqkv), F32),
                        pltpu.VMEM((N_HEADS, HEAD, HEAD), F32),
                        pltpu.VMEM((lt, d), F32), pltpu.VMEM((lt, d), F32),
                        pltpu.VMEM((lt, d), F32), pltpu.VMEM((lt, d), F32),
                        pltpu.VMEM((lt, V7X_LANES), F32), pltpu.VMEM((lt, V7X_LANES), F32),
                        pltpu.VMEM((lt, V7X_LANES), F32),
                        pltpu.VMEM((lt // CHUNK, 2 * N_HEADS, CHUNK), F32),
                        pltpu.VMEM((nb, CHUNK, 2 * CHUNK), F32),
                        pltpu.VMEM((nb, CHUNK, CHUNK), BF16), pltpu.VMEM((nb, CHUNK, 2 * HEAD), F32),
                        pltpu.VMEM((nb, CHUNK, HEAD), F32), pltpu.VMEM((nb, CHUNK, HEAD), BF16),
                        pltpu.VMEM((nb, V7X_SUBLANES, HEAD), F32), pltpu.VMEM((nb, HEAD + CHUNK, HEAD), BF16),
                        pltpu.VMEM((nb, HEAD, HEAD), F32), pltpu.VMEM((nb, CHUNK, HEAD), F32),
                        pltpu.VMEM((nb, CHUNK, 2 * HEAD), BF16)],
        compiler_params=pltpu.CompilerParams(dimension_semantics=("arbitrary", "arbitrary"),
                                             vmem_limit_bytes=V7X_VMEM_LIMIT_BYTES),
        name="prompt",
    )(x, shift, scale, gate, *consts)


def _sample_front_kernel(x_ref, shift_ref, scale_ref, sa_ref, sb_ref, npre_ref, wmain_ref, wsmall_ref,
                         caw_ref, cbw_ref, alc_ref, dtc_ref, woa_ref,
                         q_ref, k_ref, v_ref, be_ref, eg_ref, ya_ref, rest_ref, nsa_ref, nsb_ref):
    d = x_ref.shape[1]
    dqkv = 3 * d
    n = x_ref.shape[0]
    x = x_ref[...]
    h = _rms(x, npre_ref[...]) * (1.0 + scale_ref[...]) + shift_ref[...]
    hb = h.astype(BF16)

    pa = jnp.dot(hb, wmain_ref[:, 0:4 * d], preferred_element_type=F32)
    u = pa[:, d:2 * d] * pa[:, 0:d]
    conv = u * caw_ref[CONV_A_W - 1:CONV_A_W, :]
    for j in range(CONV_A_W - 1):
        conv = conv + sa_ref[:, j * d:(j + 1) * d] * caw_ref[j:j + 1, :]
    ya_ref[...] = jnp.dot(((pa[:, 2 * d:3 * d] * conv) * _silu(pa[:, 3 * d:4 * d])).astype(BF16), woa_ref[...],
                          preferred_element_type=F32)
    nsa_ref[:, 0:(CONV_A_W - 2) * d] = sa_ref[:, d:(CONV_A_W - 1) * d]
    nsa_ref[:, (CONV_A_W - 2) * d:(CONV_A_W - 1) * d] = u

    pq = jnp.dot(hb, wmain_ref[:, 4 * d:4 * d + dqkv], preferred_element_type=F32)
    cq = pq * cbw_ref[CONV_B_W - 1:CONV_B_W, :]
    for j in range(CONV_B_W - 1):
        cq = cq + sb_ref[:, j * dqkv:(j + 1) * dqkv] * cbw_ref[j:j + 1, :]
    cq = _silu(cq)
    nsb_ref[:, 0:(CONV_B_W - 2) * dqkv] = sb_ref[:, dqkv:(CONV_B_W - 1) * dqkv]
    nsb_ref[:, (CONV_B_W - 2) * dqkv:(CONV_B_W - 1) * dqkv] = pq
    q_ref[...] = _head_l2norm(cq[:, 0:d], HEAD ** -0.5)
    k_ref[...] = _head_l2norm(cq[:, d:2 * d], 1.0)
    v_ref[...] = cq[:, 2 * d:3 * d]

    rest_ref[...] = jnp.dot(hb, wmain_ref[:, 4 * d + dqkv:7 * d + dqkv], preferred_element_type=F32)

    ps = jnp.dot(hb, wsmall_ref[...], preferred_element_type=F32)
    beta = _sigmoid(ps)
    eg = jnp.exp(-jnp.exp(alc_ref[...]) * _softplus(ps + dtc_ref[...]))
    be_ref[...] = jnp.concatenate([_lane_bcast(beta, hd, HEAD) for hd in range(N_HEADS)], axis=-1)
    eg_ref[...] = jnp.concatenate([_lane_bcast(eg, N_HEADS + hd, HEAD) for hd in range(N_HEADS)], axis=-1)


def _sample_state_kernel(s_ref, q_ref, k_ref, v_ref, be_ref, eg_ref, so_ref, o_ref):
    bb = q_ref.shape[0]
    row = lax.broadcasted_iota(jnp.int32, (bb, HEAD), 0)
    erow = lax.broadcasted_iota(jnp.int32, (bb, bb * HEAD), 0)
    elane = lax.broadcasted_iota(jnp.int32, (bb, bb * HEAD), 1)
    diag = (elane // HEAD) == erow
    zeros_pad = jnp.zeros((2 * V7X_SUBLANES - bb, HEAD), F32)
    for hd in range(N_HEADS):
        sl = slice(hd * HEAD, (hd + 1) * HEAD)
        qh = q_ref[:, sl]
        kh = k_ref[:, sl]
        vh = v_ref[:, sl]
        eg = eg_ref[:, sl]
        kq = jnp.concatenate([kh, qh], axis=0).astype(BF16)
        ks = jnp.zeros((bb, HEAD), F32)
        qs = jnp.zeros((bb, HEAD), F32)
        for tk in range(bb):
            r = jnp.dot(kq, s_ref[tk, hd].astype(BF16), preferred_element_type=F32)
            ks = jnp.where(row == tk, r[0:bb], ks)
            qs = jnp.where(row == tk, r[bb:2 * bb], qs)
        vn = be_ref[:, sl] * (vh - eg * ks)
        o_ref[:, sl] = eg * qs + jnp.sum(qh * kh, axis=-1, keepdims=True) * vn
        vexp = jnp.where(diag, jnp.concatenate([vn] * bb, axis=-1), 0.0)
        outer = _dot_tn(jnp.concatenate([kh, zeros_pad], axis=0),
                        jnp.concatenate([vexp, jnp.zeros((2 * V7X_SUBLANES - bb, bb * HEAD), F32)], axis=0))
        for tk in range(bb):
            so_ref[tk, hd] = (s_ref[tk, hd] * jnp.broadcast_to(eg[tk:tk + 1, :], (HEAD, HEAD))
                              + outer[:, tk * HEAD:(tk + 1) * HEAD])


def _sample_back_kernel(x_ref, gate_ref, o_ref, ya_ref, rest_ref, onw_ref, wob_ref, wo_ref, npost_ref, y_ref):
    d = x_ref.shape[1]
    on = _head_rms_gate(o_ref[...], onw_ref[...], rest_ref[:, 0:d])
    yb = jnp.dot(on.astype(BF16), wob_ref[...], preferred_element_type=F32)
    merged = _sigmoid(rest_ref[:, d:2 * d]) * ya_ref[...] + _sigmoid(rest_ref[:, 2 * d:3 * d]) * yb
    post = _rms(jnp.dot(merged.astype(BF16), wo_ref[...], preferred_element_type=F32), npost_ref[...])
    y_ref[...] = x_ref[...] + gate_ref[...] * post


def _sample(x, shift, scale, gate, sa, sb, s0, w):
    n, d = x.shape
    dqkv = 3 * d
    vm = pl.BlockSpec(memory_space=pltpu.VMEM)
    params = pltpu.CompilerParams(vmem_limit_bytes=V7X_VMEM_LIMIT_BYTES)
    row = jax.ShapeDtypeStruct((n, d), F32)
    front_in = (x, shift, scale, sa, sb, w["npre"], w["wmain"], w["wsmall"], w["caw"], w["cbw"],
                w["alc"], w["dtc"], w["woa"])
    q, k, v, be, eg, ya, rest, nsa, nsb = pl.pallas_call(
        _sample_front_kernel,
        in_specs=[vm] * len(front_in),
        out_specs=[vm] * 9,
        out_shape=[row, row, row, row, row, row, jax.ShapeDtypeStruct((n, 3 * d), F32),
                   jax.ShapeDtypeStruct(sa.shape, F32), jax.ShapeDtypeStruct(sb.shape, F32)],
        compiler_params=params,
        name="sample_front",
    )(*front_in)

    bb = SAMPLE_BLOCK
    tok = pl.BlockSpec((bb, d), lambda i: (i, 0))
    st = pl.BlockSpec((bb, N_HEADS, HEAD, HEAD), lambda i: (i, 0, 0, 0))
    s_new, o = pl.pallas_call(
        _sample_state_kernel,
        grid=(n // bb,),
        in_specs=[st, tok, tok, tok, tok, tok],
        out_specs=[st, tok],
        out_shape=[jax.ShapeDtypeStruct(s0.shape, F32), row],
        compiler_params=pltpu.CompilerParams(dimension_semantics=("arbitrary",),
                                             vmem_limit_bytes=V7X_VMEM_LIMIT_BYTES),
        name="sample_state",
    )(s0, q, k, v, be, eg)

    back_in = (x, gate, o, ya, rest, w["onw"], w["wob"], w["wo"], w["npost"])
    y = pl.pallas_call(
        _sample_back_kernel,
        in_specs=[vm] * len(back_in),
        out_specs=vm,
        out_shape=row,
        compiler_params=params,
        name="sample_back",
    )(*back_in)
    return y, nsa, nsb, s_new


def kernel(x_prompt, x_sample, c_prompt, c_sample, state_conv_a, state_conv_qkv, state_delta, ada_w, ada_b,
           norm_pre, w_in, conv_a_w, conv_b_w, a_log, dt_bias, onorm_w, w_out_a, w_out_b, w_o, norm_post):
    depth = w_in.shape[0]
    assert depth == 1, "single-layer trunk"
    bp, seq, d = x_prompt.shape
    ns = x_sample.shape[0]
    dqkv = 3 * d
    assert d == N_HEADS * HEAD and seq % PROMPT_TILE == 0 and PROMPT_TILE % CHUNK == 0 and 2 * CHUNK == V7X_LANES
    assert ns % SAMPLE_BLOCK == 0 and x_sample.shape[1] == 1

    off_small = 4 * d + dqkv + d
    w0 = w_in[0]
    wmain = jnp.concatenate([w0[:, :off_small], w0[:, off_small + 2 * N_HEADS:]], axis=1).astype(BF16)
    wsmall_f = jnp.pad(w0[:, off_small:off_small + 2 * N_HEADS], ((0, 0), (0, V7X_LANES - 2 * N_HEADS)))
    wsmall = wsmall_f.astype(BF16)
    wsmallt = w0[:, off_small:off_small + 2 * N_HEADS].T.astype(BF16)
    zeros_h = jnp.zeros((N_HEADS,), F32)
    lane_pad = jnp.zeros((V7X_LANES - 2 * N_HEADS,), F32)
    w = {
        "npre": norm_pre, "npost": norm_post, "onw": onorm_w,
        "wmain": wmain, "wsmall": wsmall, "wsmallt": wsmallt,
        "caw": conv_a_w[0], "cbw": conv_b_w[0],
        "alc": jnp.concatenate([zeros_h, a_log[0], lane_pad])[None, :],
        "dtc": jnp.concatenate([zeros_h, dt_bias[0], lane_pad])[None, :],
        "alr": jnp.concatenate([zeros_h, a_log[0]])[:, None],
        "dtr": jnp.concatenate([zeros_h, dt_bias[0]])[:, None],
        "woa": w_out_a[0].astype(BF16), "wob": w_out_b[0].astype(BF16), "wo": w_o[0].astype(BF16),
    }

    mod = _adaln(jnp.concatenate([c_prompt, c_sample], axis=0), ada_w[0], ada_b)
    shift, scale, gate = mod[:, 0:d], mod[:, d:2 * d], mod[:, 2 * d:3 * d]

    consts = (w["npre"], w["wmain"], w["wsmall"], w["wsmallt"], w["caw"], w["cbw"], w["alc"], w["dtc"],
              w["alr"], w["dtr"], w["onw"], w["woa"], w["wob"], w["wo"], w["npost"])
    yp, pca, pcb, pds = _prompt(x_prompt, shift[:bp, None, :], scale[:bp, None, :], gate[:bp, None, :], consts)

    ys, nsa, nsb, sds = _sample(
        x_sample[:, 0, :], shift[bp:], scale[bp:], gate[bp:],
        state_conv_a[0].reshape(ns, (CONV_A_W - 1) * d), state_conv_qkv[0].reshape(ns, (CONV_B_W - 1) * dqkv),
        state_delta[0], w)

    return (yp, ys[:, None, :], pca[None], pcb[None], pds[None].astype(state_delta.dtype),
            nsa.reshape(1, ns, CONV_A_W - 1, d), nsb.reshape(1, ns, CONV_B_W - 1, dqkv),
            sds[None].astype(state_delta.dtype))
```

```python
import functools

import jax
import jax.numpy as jnp
from jax import lax
from jax.experimental import pallas as pl
from jax.experimental.pallas import tpu as pltpu

F32 = jnp.float32
BF16 = jnp.bfloat16

EPS = 1e-6
N_HEADS = 8
HEAD = 128
CONV_A_W = 3
CONV_B_W = 4

V7X_LANES = 128
V7X_SUBLANES = 8
V7X_VMEM_LIMIT_BYTES = 60 * 1024 * 1024

PROMPT_TILE = 256
SIDE_COLS = 512
SIDE_EVERY_PREP = 8
SIDE_EVERY = 16
CHUNK = 64
SAMPLE_BLOCK = 8

HI = lax.Precision.HIGHEST


NEG_LOG2_E = -1.4426950408889634


def _sigmoid(x):
    return 1.0 / (1.0 + jnp.exp2(x * NEG_LOG2_E))


def _silu(x):
    return x * _sigmoid(x)


def _softplus(x):
    return jnp.maximum(x, 0.0) + jnp.log1p(jnp.exp(-jnp.abs(x)))


def _dot(a, b):
    return jnp.dot(a.astype(BF16), b.astype(BF16), preferred_element_type=F32)


def _dot_nt(a, b):
    return lax.dot_general(a.astype(BF16), b.astype(BF16), (((1,), (1,)), ((), ())),
                           preferred_element_type=F32)


def _dot_tn(a, b):
    return lax.dot_general(a.astype(BF16), b.astype(BF16), (((0,), (0,)), ((), ())),
                           preferred_element_type=F32)


def _dot_exact(a, b):
    return jnp.dot(a, b, precision=HI, preferred_element_type=F32)


def _split(a):
    hi = a.astype(BF16)
    return hi, (a - hi.astype(F32)).astype(BF16)


def _solve_dot(a, b):
    ah, al = _split(a)
    bh, bl = _split(b)
    return (jnp.dot(ah, bh, preferred_element_type=F32) + jnp.dot(ah, bl, preferred_element_type=F32)
            + jnp.dot(al, bh, preferred_element_type=F32))


def _rms(x, w):
    return x * lax.rsqrt(jnp.mean(x * x, axis=-1, keepdims=True) + EPS) * w


def _head_l2norm(x, scale):
    outs = []
    for h in range(x.shape[1] // HEAD):
        xh = x[:, h * HEAD:(h + 1) * HEAD]
        outs.append(xh * (lax.rsqrt(jnp.sum(xh * xh, axis=-1, keepdims=True) + EPS) * scale))
    return jnp.concatenate(outs, axis=-1)


def _head_rms_gate(o, w, z):
    outs = []
    for h in range(N_HEADS):
        sl = slice(h * HEAD, (h + 1) * HEAD)
        oh = o[:, sl]
        outs.append(oh * lax.rsqrt(jnp.mean(oh * oh, axis=-1, keepdims=True) + EPS) * w * _silu(z[:, sl]))
    return jnp.concatenate(outs, axis=-1)


def _lane_bcast(x, col, width):
    return jnp.broadcast_to(x[:, col:col + 1], (x.shape[0], width))


def _adaln_kernel(c_ref, w_ref, b_ref, o_ref):
    o_ref[...] = _dot(_silu(c_ref[...]), w_ref[...]) + b_ref[...]


def _adaln(c_all, ada_w, ada_b):
    n, d = c_all.shape
    cols = ada_w.shape[1]
    return pl.pallas_call(
        _adaln_kernel,
        grid=(cols // d,),
        in_specs=[pl.BlockSpec((n, d), lambda j: (0, 0)),
                  pl.BlockSpec((d, d), lambda j: (0, j)),
                  pl.BlockSpec((1, d), lambda j: (0, j))],
        out_specs=pl.BlockSpec((n, d), lambda j: (0, j)),
        out_shape=jax.ShapeDtypeStruct((n, cols), F32),
        name="adaln",
    )(c_all, ada_w, ada_b)


def _prompt_kernel(x_ref, shift_ref, scale_ref, gate_ref, npre_ref, wmain_ref, wsmall_ref, wsmallt_ref,
                   caw_ref, cbw_ref, alc_ref, dtc_ref, alr_ref, dtr_ref, onw_ref, woa_ref, wob_ref, wo_ref,
                   npost_ref,
                   y_ref, ca_ref, cb_ref, s_ref,
                   abuf, qkvbuf, s_scr, q_scr, k_scr, v_scr, o_scr, b_scr, gc_scr, gt_scr, gcr_scr,
                   r_scr, aqk_scr, rhs_scr, qd_scr, kd_scr, egt_scr, mp_scr, sb_scr, ob_scr, wu_scr,
                   pa_scr, pr_scr, yap_scr, ya_scr):
    lt = x_ref.shape[1]
    d = x_ref.shape[2]
    dqkv = qkvbuf.shape[1]
    n_chunks = lt // CHUNK
    pad = V7X_SUBLANES
    t = pl.program_id(1)

    @pl.when(t == 0)
    def _():
        abuf[0:pad, :] = jnp.zeros((pad, d), F32)
        qkvbuf[0:pad, :] = jnp.zeros((pad, dqkv), F32)
        s_scr[...] = jnp.zeros(s_scr.shape, F32)

    x = x_ref[0]
    h = _rms(x, npre_ref[...]) * (1.0 + scale_ref[0]) + shift_ref[0]
    hb = h.astype(BF16)

    side_work = []

    def side_proj(dst_ref, w_off, blk):
        def run():
            cols = slice(blk * SIDE_COLS, (blk + 1) * SIDE_COLS)
            dst_ref[:, cols] = jnp.dot(hb, wmain_ref[:, w_off + blk * SIDE_COLS:w_off + (blk + 1) * SIDE_COLS],
                                       preferred_element_type=F32)
        return run

    def side_branch_a(blk):
        def run():
            cols = slice(blk * SIDE_COLS, (blk + 1) * SIDE_COLS)
            u = pa_scr[:, d + blk * SIDE_COLS:d + (blk + 1) * SIDE_COLS] * pa_scr[:, cols]
            abuf[pad:pad + lt, cols] = u
            ext = abuf[:, cols]
            conv = u * caw_ref[CONV_A_W - 1:CONV_A_W, cols]
            for j in range(CONV_A_W - 1):
                conv = conv + pltpu.roll(ext, CONV_A_W - 1 - j, axis=0)[pad:pad + lt] * caw_ref[j:j + 1, cols]
            b_a = pa_scr[:, 2 * d + blk * SIDE_COLS:2 * d + (blk + 1) * SIDE_COLS]
            z_a = pa_scr[:, 3 * d + blk * SIDE_COLS:3 * d + (blk + 1) * SIDE_COLS]
            yap_scr[:, cols] = ((b_a * conv) * _silu(z_a)).astype(BF16)
            ca_ref[0, :, cols] = abuf[pad + lt - (CONV_A_W - 1):pad + lt, cols]
            abuf[0:pad, cols] = abuf[lt:lt + pad, cols]
        return run

    def side_out_a(blk):
        def run():
            cols = slice(blk * SIDE_COLS, (blk + 1) * SIDE_COLS)
            ya_scr[:, cols] = jnp.dot(yap_scr[...], woa_ref[:, cols], preferred_element_type=F32)
        return run

    side_work += [side_proj(pa_scr, 0, blk) for blk in range(4 * d // SIDE_COLS)]
    side_work += [side_branch_a(blk) for blk in range(d // SIDE_COLS)]
    side_work += [side_out_a(blk) for blk in range(d // SIDE_COLS)]
    side_work += [side_proj(pr_scr, 4 * d + dqkv, blk) for blk in range(3 * d // SIDE_COLS)]
    side_work.reverse()

    def tick():
        if side_work:
            side_work.pop()()

    def proj_qkv(blk):
        cols = slice(blk * SIDE_COLS, (blk + 1) * SIDE_COLS)
        qkvbuf[pad:pad + lt, cols] = jnp.dot(hb, wmain_ref[:, 4 * d + blk * SIDE_COLS:4 * d + (blk + 1) * SIDE_COLS],
                                             preferred_element_type=F32)

    proj_qkv(0)
    for blk in range(dqkv // SIDE_COLS):
        if blk + 1 < dqkv // SIDE_COLS:
            proj_qkv(blk + 1)
        else:
            tick()
        cols = slice(blk * SIDE_COLS, (blk + 1) * SIDE_COLS)
        ext = qkvbuf[:, cols]
        cq = ext[pad:pad + lt] * cbw_ref[CONV_B_W - 1:CONV_B_W, cols]
        for j in range(CONV_B_W - 1):
            cq = cq + pltpu.roll(ext, CONV_B_W - 1 - j, axis=0)[pad:pad + lt] * cbw_ref[j:j + 1, cols]
        cq = _silu(cq)
        cb_ref[0, :, cols] = qkvbuf[pad + lt - (CONV_B_W - 1):pad + lt, cols]
        qkvbuf[0:pad, cols] = qkvbuf[lt:lt + pad, cols]
        dst = slice((blk * SIDE_COLS) % d, (blk * SIDE_COLS) % d + SIDE_COLS)
        if blk * SIDE_COLS < d:
            q_scr[:, dst] = _head_l2norm(cq, HEAD ** -0.5)
        elif blk * SIDE_COLS < 2 * d:
            k_scr[:, dst] = _head_l2norm(cq, 1.0)
        else:
            v_scr[:, dst] = cq

    ps = jnp.dot(hb, wsmall_ref[...], preferred_element_type=F32)
    pst = lax.dot_general(wsmallt_ref[...], hb, (((1,), (1,)), ((), ())), preferred_element_type=F32)
    b_scr[...] = _sigmoid(ps)
    g_col = -jnp.exp(alc_ref[...]) * _softplus(ps + dtc_ref[...])
    g_row = -jnp.exp(alr_ref[...]) * _softplus(pst + dtr_ref[...])
    ri = lax.broadcasted_iota(jnp.int32, (lt, lt), 0)
    ci = lax.broadcasted_iota(jnp.int32, (lt, lt), 1)
    same = (ri // CHUNK) == (ci // CHUNK)
    gc_scr[...] = _dot_exact(jnp.where(same & (ci <= ri), 1.0, 0.0).astype(F32), g_col)
    gt_scr[...] = _dot_exact(jnp.where(same, 1.0, 0.0).astype(F32), g_col)
    rj = lax.broadcasted_iota(jnp.int32, (CHUNK, CHUNK), 0)
    cj = lax.broadcasted_iota(jnp.int32, (CHUNK, CHUNK), 1)
    triu = jnp.where(rj <= cj, 1.0, 0.0).astype(F32)
    for c in range(n_chunks):
        gcr_scr[c] = _dot_exact(g_row[:, c * CHUNK:(c + 1) * CHUNK], triu)
    tick()

    causal = rj >= cj
    strict = rj > cj
    eye = jnp.where(rj == cj, 1.0, 0.0).astype(F32)

    nb = n_chunks * N_HEADS
    lane = lax.broadcasted_iota(jnp.int32, (CHUNK, 2 * CHUNK), 1)
    low_half = lane < CHUNK

    for c in range(n_chunks):
        rows = slice(c * CHUNK, (c + 1) * CHUNK)
        beta_blk = b_scr[rows, :]
        gc_blk = gc_scr[rows, :]
        gt_blk = gt_scr[rows, :]
        gcr_blk = gcr_scr[c]
        for hd in range(N_HEADS):
            n = c * N_HEADS + hd
            sl = slice(hd * HEAD, (hd + 1) * HEAD)
            qh = q_scr[rows, sl]
            kh = k_scr[rows, sl]
            vh = v_scr[rows, sl]
            bcol = _lane_bcast(beta_blk, hd, HEAD)
            gcol = _lane_bcast(gc_blk, N_HEADS + hd, HEAD)
            gtot = _lane_bcast(gt_blk, N_HEADS + hd, HEAD)
            grow = gcr_blk[N_HEADS + hd:N_HEADS + hd + 1, :]
            decay = jnp.exp(jnp.where(causal, gcol[:, 0:CHUNK] - grow, -jnp.inf))
            egc = jnp.exp(gcol)
            kb = kh * bcol
            xm = -jnp.where(strict, _dot_nt(kb, kh) * decay, 0.0)
            r_scr[n] = jnp.concatenate([xm, eye], axis=-1)
            aqk_scr[n] = (_dot_nt(qh, kh) * decay).astype(BF16)
            rhs_scr[n] = jnp.concatenate([vh * bcol, kb * egc], axis=-1)
            qd_scr[n] = qh * egc
            kd_scr[n] = (kh * jnp.exp(gtot - gcol)).astype(BF16)
            egt_scr[n] = jnp.exp(gtot[0:V7X_SUBLANES, :])
            if n % SIDE_EVERY_PREP == SIDE_EVERY_PREP - 1:
                tick()

    def split_parts(r):
        hi = r.astype(BF16)
        hi32 = hi.astype(F32)
        lo32 = r - hi32
        return hi, hi32, lo32

    m = 1
    while m < CHUNK:
        for n in range(nb):
            r = r_scr[n]
            hi, hi32, lo32 = split_parts(r)
            lhs = jnp.where(low_half, hi32, pltpu.roll(lo32, CHUNK, axis=1)).astype(BF16)
            lo = lo32.astype(BF16)
            prod = jnp.dot(jnp.concatenate([lhs, lhs], axis=1), jnp.concatenate([hi, hi, lo, lo], axis=0),
                           preferred_element_type=F32)
            r_scr[n] = prod + jnp.where(low_half, 0.0, r)
            if n % SIDE_EVERY == SIDE_EVERY - 1:
                tick()
        m *= 2
    for n in range(nb):
        hi, hi32, lo32 = split_parts(r_scr[n])
        lhs = jnp.where(low_half, pltpu.roll(lo32, CHUNK, axis=1), hi32).astype(BF16)
        bh, _, bl32 = split_parts(rhs_scr[n])
        bl = bl32.astype(BF16)
        sol = jnp.dot(jnp.concatenate([lhs, lhs], axis=1), jnp.concatenate([bh, bh, bl, bl], axis=0),
                      preferred_element_type=F32)
        wu_scr[n] = jnp.concatenate([sol[:, HEAD:2 * HEAD], sol[:, 0:HEAD]], axis=1).astype(BF16)
        if n % SIDE_EVERY == SIDE_EVERY - 1:
            tick()
    for n in range(nb):
        wu = wu_scr[n]
        kw = lax.dot_general(kd_scr[n], wu, (((0,), (0,)), ((), ())), preferred_element_type=F32)
        aw = jnp.dot(aqk_scr[n], wu, preferred_element_type=F32)
        mp_scr[n, 0:HEAD, :] = kw[:, 0:HEAD].astype(BF16)
        mp_scr[n, HEAD:HEAD + CHUNK, :] = (qd_scr[n] - aw[:, 0:HEAD]).astype(BF16)
        sb_scr[n] = kw[:, HEAD:2 * HEAD]
        ob_scr[n] = aw[:, HEAD:2 * HEAD]
        if n % SIDE_EVERY == SIDE_EVERY - 1:
            tick()

    for c in range(n_chunks):
        rows = slice(c * CHUNK, (c + 1) * CHUNK)
        for hd in range(N_HEADS):
            n = c * N_HEADS + hd
            sl = slice(hd * HEAD, (hd + 1) * HEAD)
            s_old = s_scr[hd]
            res = jnp.dot(mp_scr[n], s_old.astype(BF16), preferred_element_type=F32)
            o_scr[rows, sl] = res[HEAD:HEAD + CHUNK] + ob_scr[n]
            s_scr[hd] = (s_old * jnp.concatenate([egt_scr[n]] * (HEAD // V7X_SUBLANES), axis=0)
                         - res[0:HEAD] + sb_scr[n])
        tick()
    while side_work:
        tick()

    on = _head_rms_gate(o_scr[...], onw_ref[...], pr_scr[:, 0:d])
    yb = jnp.dot(on.astype(BF16), wob_ref[...], preferred_element_type=F32)
    merged = _sigmoid(pr_scr[:, d:2 * d]) * ya_scr[...] + _sigmoid(pr_scr[:, 2 * d:3 * d]) * yb
    post = _rms(jnp.dot(merged.astype(BF16), wo_ref[...], preferred_element_type=F32), npost_ref[...])
    y_ref[0] = x + gate_ref[0] * post

    @pl.when(t == pl.num_programs(1) - 1)
    def _():
        s_ref[0] = s_scr[...]


def _prompt(x, shift, scale, gate, consts):
    bsz, seq, d = x.shape
    dqkv = 3 * d
    lt = PROMPT_TILE
    pad = V7X_SUBLANES
    nb = (lt // CHUNK) * N_HEADS
    vm = pl.BlockSpec(memory_space=pltpu.VMEM)
    per_b = pl.BlockSpec((1, 1, d), lambda b, t: (b, 0, 0))
    return pl.pallas_call(
        _prompt_kernel,
        grid=(bsz, seq // lt),
        in_specs=[pl.BlockSpec((1, lt, d), lambda b, t: (b, t, 0)), per_b, per_b, per_b] + [vm] * len(consts),
        out_specs=[pl.BlockSpec((1, lt, d), lambda b, t: (b, t, 0)),
                   pl.BlockSpec((1, CONV_A_W - 1, d), lambda b, t: (b, 0, 0)),
                   pl.BlockSpec((1, CONV_B_W - 1, dqkv), lambda b, t: (b, 0, 0)),
                   pl.BlockSpec((1, N_HEADS, HEAD, HEAD), lambda b, t: (b, 0, 0, 0))],
        out_shape=[jax.ShapeDtypeStruct((bsz, seq, d), F32),
                   jax.ShapeDtypeStruct((bsz, CONV_A_W - 1, d), F32),
                   jax.ShapeDtypeStruct((bsz, CONV_B_W - 1, dqkv), F32),
                   jax.ShapeDtypeStruct((bsz, N_HEADS, HEAD, HEAD), F32)],
        scratch_shapes=[pltpu.VMEM((pad + lt, d), F32),
                        pltpu.VMEM((pad + lt, dqkv), F32),
                        pltpu.VMEM((N_HEADS, HEAD, HEAD), F32),
                        pltpu.VMEM((lt, d), F32), pltpu.VMEM((lt, d), F32),
                        pltpu.VMEM((lt, d), F32), pltpu.VMEM((lt, d), F32),
                        pltpu.VMEM((lt, V7X_LANES), F32), pltpu.VMEM((lt, V7X_LANES), F32),
                        pltpu.VMEM((lt, V7X_LANES), F32),
                        pltpu.VMEM((lt // CHUNK, 2 * N_HEADS, CHUNK), F32),
                        pltpu.VMEM((nb, CHUNK, 2 * CHUNK), F32),
                        pltpu.VMEM((nb, CHUNK, CHUNK), BF16), pltpu.VMEM((nb, CHUNK, 2 * HEAD), F32),
                        pltpu.VMEM((nb, CHUNK, HEAD), F32), pltpu.VMEM((nb, CHUNK, HEAD), BF16),
                        pltpu.VMEM((nb, V7X_SUBLANES, HEAD), F32), pltpu.VMEM((nb, HEAD + CHUNK, HEAD), BF16),
                        pltpu.VMEM((nb, HEAD, HEAD), F32), pltpu.VMEM((nb, CHUNK, HEAD), F32),
                        pltpu.VMEM((nb, CHUNK, 2 * HEAD), BF16),
                        pltpu.VMEM((lt, 4 * d), F32), pltpu.VMEM((lt, 3 * d), F32),
                        pltpu.VMEM((lt, d), BF16), pltpu.VMEM((lt, d), F32)],
        compiler_params=pltpu.CompilerParams(dimension_semantics=("arbitrary", "arbitrary"),
                                             vmem_limit_bytes=V7X_VMEM_LIMIT_BYTES),
        name="prompt",
    )(x, shift, scale, gate, *consts)


def _sample_front_kernel(x_ref, shift_ref, scale_ref, sa_ref, sb_ref, npre_ref, wmain_ref, wsmall_ref,
                         caw_ref, cbw_ref, alc_ref, dtc_ref, woa_ref,
                         q_ref, k_ref, v_ref, be_ref, eg_ref, ya_ref, rest_ref, nsa_ref, nsb_ref):
    d = x_ref.shape[1]
    dqkv = 3 * d
    n = x_ref.shape[0]
    x = x_ref[...]
    h = _rms(x, npre_ref[...]) * (1.0 + scale_ref[...]) + shift_ref[...]
    hb = h.astype(BF16)

    pa = jnp.dot(hb, wmain_ref[:, 0:4 * d], preferred_element_type=F32)
    u = pa[:, d:2 * d] * pa[:, 0:d]
    conv = u * caw_ref[CONV_A_W - 1:CONV_A_W, :]
    for j in range(CONV_A_W - 1):
        conv = conv + sa_ref[:, j * d:(j + 1) * d] * caw_ref[j:j + 1, :]
    ya_ref[...] = jnp.dot(((pa[:, 2 * d:3 * d] * conv) * _silu(pa[:, 3 * d:4 * d])).astype(BF16), woa_ref[...],
                          preferred_element_type=F32)
    nsa_ref[:, 0:(CONV_A_W - 2) * d] = sa_ref[:, d:(CONV_A_W - 1) * d]
    nsa_ref[:, (CONV_A_W - 2) * d:(CONV_A_W - 1) * d] = u

    pq = jnp.dot(hb, wmain_ref[:, 4 * d:4 * d + dqkv], preferred_element_type=F32)
    cq = pq * cbw_ref[CONV_B_W - 1:CONV_B_W, :]
    for j in range(CONV_B_W - 1):
        cq = cq + sb_ref[:, j * dqkv:(j + 1) * dqkv] * cbw_ref[j:j + 1, :]
    cq = _silu(cq)
    nsb_ref[:, 0:(CONV_B_W - 2) * dqkv] = sb_ref[:, dqkv:(CONV_B_W - 1) * dqkv]
    nsb_ref[:, (CONV_B_W - 2) * dqkv:(CONV_B_W - 1) * dqkv] = pq
    q_ref[...] = _head_l2norm(cq[:, 0:d], HEAD ** -0.5)
    k_ref[...] = _head_l2norm(cq[:, d:2 * d], 1.0)
    v_ref[...] = cq[:, 2 * d:3 * d]

    rest_ref[...] = jnp.dot(hb, wmain_ref[:, 4 * d + dqkv:7 * d + dqkv], preferred_element_type=F32)

    ps = jnp.dot(hb, wsmall_ref[...], preferred_element_type=F32)
    beta = _sigmoid(ps)
    eg = jnp.exp(-jnp.exp(alc_ref[...]) * _softplus(ps + dtc_ref[...]))
    be_ref[...] = jnp.concatenate([_lane_bcast(beta, hd, HEAD) for hd in range(N_HEADS)], axis=-1)
    eg_ref[...] = jnp.concatenate([_lane_bcast(eg, N_HEADS + hd, HEAD) for hd in range(N_HEADS)], axis=-1)


def _sample_state_kernel(s_ref, q_ref, k_ref, v_ref, be_ref, eg_ref, so_ref, o_ref):
    bb = q_ref.shape[0]
    row = lax.broadcasted_iota(jnp.int32, (bb, HEAD), 0)
    erow = lax.broadcasted_iota(jnp.int32, (bb, bb * HEAD), 0)
    elane = lax.broadcasted_iota(jnp.int32, (bb, bb * HEAD), 1)
    diag = (elane // HEAD) == erow
    zeros_pad = jnp.zeros((2 * V7X_SUBLANES - bb, HEAD), F32)
    for hd in range(N_HEADS):
        sl = slice(hd * HEAD, (hd + 1) * HEAD)
        qh = q_ref[:, sl]
        kh = k_ref[:, sl]
        vh = v_ref[:, sl]
        eg = eg_ref[:, sl]
        kq = jnp.concatenate([kh, qh], axis=0).astype(BF16)
        ks = jnp.zeros((bb, HEAD), F32)
        qs = jnp.zeros((bb, HEAD), F32)
        for tk in range(bb):
            r = jnp.dot(kq, s_ref[tk, hd].astype(BF16), preferred_element_type=F32)
            ks = jnp.where(row == tk, r[0:bb], ks)
            qs = jnp.where(row == tk, r[bb:2 * bb], qs)
        vn = be_ref[:, sl] * (vh - eg * ks)
        o_ref[:, sl] = eg * qs + jnp.sum(qh * kh, axis=-1, keepdims=True) * vn
        vexp = jnp.where(diag, jnp.concatenate([vn] * bb, axis=-1), 0.0)
        outer = _dot_tn(jnp.concatenate([kh, zeros_pad], axis=0),
                        jnp.concatenate([vexp, jnp.zeros((2 * V7X_SUBLANES - bb, bb * HEAD), F32)], axis=0))
        for tk in range(bb):
            so_ref[tk, hd] = (s_ref[tk, hd] * jnp.broadcast_to(eg[tk:tk + 1, :], (HEAD, HEAD))
                              + outer[:, tk * HEAD:(tk + 1) * HEAD])


def _sample_back_kernel(x_ref, gate_ref, o_ref, ya_ref, rest_ref, onw_ref, wob_ref, wo_ref, npost_ref, y_ref):
    d = x_ref.shape[1]
    on = _head_rms_gate(o_ref[...], onw_ref[...], rest_ref[:, 0:d])
    yb = jnp.dot(on.astype(BF16), wob_ref[...], preferred_element_type=F32)
    merged = _sigmoid(rest_ref[:, d:2 * d]) * ya_ref[...] + _sigmoid(rest_ref[:, 2 * d:3 * d]) * yb
    post = _rms(jnp.dot(merged.astype(BF16), wo_ref[...], preferred_element_type=F32), npost_ref[...])
    y_ref[...] = x_ref[...] + gate_ref[...] * post


def _sample(x, shift, scale, gate, sa, sb, s0, w):
    n, d = x.shape
    dqkv = 3 * d
    vm = pl.BlockSpec(memory_space=pltpu.VMEM)
    params = pltpu.CompilerParams(vmem_limit_bytes=V7X_VMEM_LIMIT_BYTES)
    row = jax.ShapeDtypeStruct((n, d), F32)
    front_in = (x, shift, scale, sa, sb, w["npre"], w["wmain"], w["wsmall"], w["caw"], w["cbw"],
                w["alc"], w["dtc"], w["woa"])
    q, k, v, be, eg, ya, rest, nsa, nsb = pl.pallas_call(
        _sample_front_kernel,
        in_specs=[vm] * len(front_in),
        out_specs=[vm] * 9,
        out_shape=[row, row, row, row, row, row, jax.ShapeDtypeStruct((n, 3 * d), F32),
                   jax.ShapeDtypeStruct(sa.shape, F32), jax.ShapeDtypeStruct(sb.shape, F32)],
        compiler_params=params,
        name="sample_front",
    )(*front_in)

    bb = SAMPLE_BLOCK
    tok = pl.BlockSpec((bb, d), lambda i: (i, 0))
    st = pl.BlockSpec((bb, N_HEADS, HEAD, HEAD), lambda i: (i, 0, 0, 0))
    s_new, o = pl.pallas_call(
        _sample_state_kernel,
        grid=(n // bb,),
        in_specs=[st, tok, tok, tok, tok, tok],
        out_specs=[st, tok],
        out_shape=[jax.ShapeDtypeStruct(s0.shape, F32), row],
        compiler_params=pltpu.CompilerParams(dimension_semantics=("arbitrary",),
                                             vmem_limit_bytes=V7X_VMEM_LIMIT_BYTES),
        name="sample_state",
    )(s0, q, k, v, be, eg)

    back_in = (x, gate, o, ya, rest, w["onw"], w["wob"], w["wo"], w["npost"])
    y = pl.pallas_call(
        _sample_back_kernel,
        in_specs=[vm] * len(back_in),
        out_specs=vm,
        out_shape=row,
        compiler_params=params,
        name="sample_back",
    )(*back_in)
    return y, nsa, nsb, s_new


def kernel(x_prompt, x_sample, c_prompt, c_sample, state_conv_a, state_conv_qkv, state_delta, ada_w, ada_b,
           norm_pre, w_in, conv_a_w, conv_b_w, a_log, dt_bias, onorm_w, w_out_a, w_out_b, w_o, norm_post):
    depth = w_in.shape[0]
    assert depth == 1, "single-layer trunk"
    bp, seq, d = x_prompt.shape
    ns = x_sample.shape[0]
    dqkv = 3 * d
    assert d == N_HEADS * HEAD and seq % PROMPT_TILE == 0 and PROMPT_TILE % CHUNK == 0 and 2 * CHUNK == V7X_LANES
    assert ns % SAMPLE_BLOCK == 0 and x_sample.shape[1] == 1

    off_small = 4 * d + dqkv + d
    w0 = w_in[0]
    wmain = jnp.concatenate([w0[:, :off_small], w0[:, off_small + 2 * N_HEADS:]], axis=1).astype(BF16)
    wsmall_f = jnp.pad(w0[:, off_small:off_small + 2 * N_HEADS], ((0, 0), (0, V7X_LANES - 2 * N_HEADS)))
    wsmall = wsmall_f.astype(BF16)
    wsmallt = w0[:, off_small:off_small + 2 * N_HEADS].T.astype(BF16)
    zeros_h = jnp.zeros((N_HEADS,), F32)
    lane_pad = jnp.zeros((V7X_LANES - 2 * N_HEADS,), F32)
    w = {
        "npre": norm_pre, "npost": norm_post, "onw": onorm_w,
        "wmain": wmain, "wsmall": wsmall, "wsmallt": wsmallt,
        "caw": conv_a_w[0], "cbw": conv_b_w[0],
        "alc": jnp.concatenate([zeros_h, a_log[0], lane_pad])[None, :],
        "dtc": jnp.concatenate([zeros_h, dt_bias[0], lane_pad])[None, :],
        "alr": jnp.concatenate([zeros_h, a_log[0]])[:, None],
        "dtr": jnp.concatenate([zeros_h, dt_bias[0]])[:, None],
        "woa": w_out_a[0].astype(BF16), "wob": w_out_b[0].astype(BF16), "wo": w_o[0].astype(BF16),
    }

    mod = _adaln(jnp.concatenate([c_prompt, c_sample], axis=0), ada_w[0], ada_b)
    shift, scale, gate = mod[:, 0:d], mod[:, d:2 * d], mod[:, 2 * d:3 * d]

    consts = (w["npre"], w["wmain"], w["wsmall"], w["wsmallt"], w["caw"], w["cbw"], w["alc"], w["dtc"],
              w["alr"], w["dtr"], w["onw"], w["woa"], w["wob"], w["wo"], w["npost"])
    yp, pca, pcb, pds = _prompt(x_prompt, shift[:bp, None, :], scale[:bp, None, :], gate[:bp, None, :], consts)

    ys, nsa, nsb, sds = _sample(
        x_sample[:, 0, :], shift[bp:], scale[bp:], gate[bp:],
        state_conv_a[0].reshape(ns, (CONV_A_W - 1) * d), state_conv_qkv[0].reshape(ns, (CONV_B_W - 1) * dqkv),
        state_delta[0], w)

    return (yp, ys[:, None, :], pca[None], pcb[None], pds[None].astype(state_delta.dtype),
            nsa.reshape(1, ns, CONV_A_W - 1, d), nsb.reshape(1, ns, CONV_B_W - 1, dqkv),
            sds[None].astype(state_delta.dtype))
```

```python
import functools

import jax
import jax.numpy as jnp
from jax import lax
from jax.experimental import pallas as pl
from jax.experimental.pallas import tpu as pltpu

F32 = jnp.float32
BF16 = jnp.bfloat16

EPS = 1e-6
N_HEADS = 8
HEAD = 128
CONV_A_W = 3
CONV_B_W = 4

V7X_LANES = 128
V7X_SUBLANES = 8
V7X_VMEM_LIMIT_BYTES = 60 * 1024 * 1024

PROMPT_TILE = 256
SIDE_COLS = 512
SIDE_EVERY_PREP = 8
SIDE_EVERY = 16
CHUNK = 64
SAMPLE_BLOCK = 8

HI = lax.Precision.HIGHEST


NEG_LOG2_E = -1.4426950408889634


def _sigmoid(x):
    return 1.0 / (1.0 + jnp.exp2(x * NEG_LOG2_E))


def _silu(x):
    return x * _sigmoid(x)


def _softplus(x):
    return jnp.maximum(x, 0.0) + jnp.log1p(jnp.exp(-jnp.abs(x)))


def _dot(a, b):
    return jnp.dot(a.astype(BF16), b.astype(BF16), preferred_element_type=F32)


def _dot_nt(a, b):
    return lax.dot_general(a.astype(BF16), b.astype(BF16), (((1,), (1,)), ((), ())),
                           preferred_element_type=F32)


def _dot_tn(a, b):
    return lax.dot_general(a.astype(BF16), b.astype(BF16), (((0,), (0,)), ((), ())),
                           preferred_element_type=F32)


def _dot_exact(a, b):
    return jnp.dot(a, b, precision=HI, preferred_element_type=F32)


def _split(a):
    hi = a.astype(BF16)
    return hi, (a - hi.astype(F32)).astype(BF16)


def _solve_dot(a, b):
    ah, al = _split(a)
    bh, bl = _split(b)
    return (jnp.dot(ah, bh, preferred_element_type=F32) + jnp.dot(ah, bl, preferred_element_type=F32)
            + jnp.dot(al, bh, preferred_element_type=F32))


def _rms(x, w):
    return x * lax.rsqrt(jnp.mean(x * x, axis=-1, keepdims=True) + EPS) * w


def _head_l2norm(x, scale):
    outs = []
    for h in range(x.shape[1] // HEAD):
        xh = x[:, h * HEAD:(h + 1) * HEAD]
        outs.append(xh * (lax.rsqrt(jnp.sum(xh * xh, axis=-1, keepdims=True) + EPS) * scale))
    return jnp.concatenate(outs, axis=-1)


def _head_rms_gate(o, w, z):
    outs = []
    for h in range(N_HEADS):
        sl = slice(h * HEAD, (h + 1) * HEAD)
        oh = o[:, sl]
        outs.append(oh * lax.rsqrt(jnp.mean(oh * oh, axis=-1, keepdims=True) + EPS) * w * _silu(z[:, sl]))
    return jnp.concatenate(outs, axis=-1)


def _lane_bcast(x, col, width):
    return jnp.broadcast_to(x[:, col:col + 1], (x.shape[0], width))


def _adaln_kernel(c_ref, w_ref, b_ref, o_ref):
    o_ref[...] = _dot(_silu(c_ref[...]), w_ref[...]) + b_ref[...]


def _adaln(c_all, ada_w, ada_b):
    n, d = c_all.shape
    cols = ada_w.shape[1]
    return pl.pallas_call(
        _adaln_kernel,
        grid=(cols // d,),
        in_specs=[pl.BlockSpec((n, d), lambda j: (0, 0)),
                  pl.BlockSpec((d, d), lambda j: (0, j)),
                  pl.BlockSpec((1, d), lambda j: (0, j))],
        out_specs=pl.BlockSpec((n, d), lambda j: (0, j)),
        out_shape=jax.ShapeDtypeStruct((n, cols), F32),
        name="adaln",
    )(c_all, ada_w, ada_b)


def _prompt_kernel(x_ref, shift_ref, scale_ref, gate_ref, npre_ref, wmain_ref, wgate_ref, wsmall_ref, wsmallt_ref,
                   caw_ref, cbw_ref, alc_ref, dtc_ref, alr_ref, dtr_ref, onw_ref, woa_ref, wob_ref, wo_ref,
                   npost_ref,
                   y_ref, ca_ref, cb_ref, s_ref,
                   abuf, qkvbuf, s_scr, q_scr, k_scr, v_scr, o_scr, b_scr, gc_scr, gt_scr, gcr_scr,
                   r_scr, aqk_scr, rhs_scr, qd_scr, kd_scr, egt_scr, mp_scr, sb_scr, ob_scr, wu_scr,
                   pa_scr, pr_scr, yap_scr, ya_scr):
    lt = x_ref.shape[1]
    d = x_ref.shape[2]
    dqkv = qkvbuf.shape[1]
    n_chunks = lt // CHUNK
    pad = V7X_SUBLANES
    t = pl.program_id(1)

    @pl.when(t == 0)
    def _():
        abuf[0:pad, :] = jnp.zeros((pad, d), F32)
        qkvbuf[0:pad, :] = jnp.zeros((pad, dqkv), F32)
        s_scr[...] = jnp.zeros(s_scr.shape, F32)

    x = x_ref[0]
    h = _rms(x, npre_ref[...]) * (1.0 + scale_ref[0]) + shift_ref[0]
    hb = h.astype(BF16)

    side_work = []

    def side_proj(dst_ref, dst_off, w_ref, w_off, blk):
        def run():
            dst_ref[:, dst_off + blk * SIDE_COLS:dst_off + (blk + 1) * SIDE_COLS] = jnp.dot(
                hb, w_ref[:, w_off + blk * SIDE_COLS:w_off + (blk + 1) * SIDE_COLS], preferred_element_type=F32)
        return run

    def side_branch_a(blk):
        def run():
            cols = slice(blk * SIDE_COLS, (blk + 1) * SIDE_COLS)
            u = pa_scr[:, d + blk * SIDE_COLS:d + (blk + 1) * SIDE_COLS] * pa_scr[:, cols]
            abuf[pad:pad + lt, cols] = u
            ext = abuf[:, cols]
            conv = u * caw_ref[CONV_A_W - 1:CONV_A_W, cols]
            for j in range(CONV_A_W - 1):
                conv = conv + pltpu.roll(ext, CONV_A_W - 1 - j, axis=0)[pad:pad + lt] * caw_ref[j:j + 1, cols]
            b_a = pa_scr[:, 2 * d + blk * SIDE_COLS:2 * d + (blk + 1) * SIDE_COLS]
            z_a = pa_scr[:, 3 * d + blk * SIDE_COLS:3 * d + (blk + 1) * SIDE_COLS]
            yap_scr[:, cols] = ((b_a * conv) * _silu(z_a)).astype(BF16)
            ca_ref[0, :, cols] = abuf[pad + lt - (CONV_A_W - 1):pad + lt, cols]
            abuf[0:pad, cols] = abuf[lt:lt + pad, cols]
        return run

    def side_out_a(blk):
        def run():
            cols = slice(blk * SIDE_COLS, (blk + 1) * SIDE_COLS)
            ya_scr[:, cols] = jnp.dot(yap_scr[...], woa_ref[:, cols], preferred_element_type=F32)
        return run

    side_work += [side_proj(pa_scr, 0, wmain_ref, 0, blk) for blk in range(4 * d // SIDE_COLS)]
    side_work += [side_branch_a(blk) for blk in range(d // SIDE_COLS)]
    side_work += [side_out_a(blk) for blk in range(d // SIDE_COLS)]
    side_work += [side_proj(pr_scr, 0, wmain_ref, 4 * d + dqkv, blk) for blk in range(d // SIDE_COLS)]
    side_work += [side_proj(pr_scr, d, wgate_ref, 0, blk) for blk in range(2 * d // SIDE_COLS)]
    side_work.reverse()

    def tick():
        if side_work:
            side_work.pop()()

    def conv_qkv(blk):
        cols = slice(blk * SIDE_COLS, (blk + 1) * SIDE_COLS)
        ext = qkvbuf[:, cols]
        cq = ext[pad:pad + lt] * cbw_ref[CONV_B_W - 1:CONV_B_W, cols]
        for j in range(CONV_B_W - 1):
            cq = cq + pltpu.roll(ext, CONV_B_W - 1 - j, axis=0)[pad:pad + lt] * cbw_ref[j:j + 1, cols]
        cq = _silu(cq)
        cb_ref[0, :, cols] = qkvbuf[pad + lt - (CONV_B_W - 1):pad + lt, cols]
        qkvbuf[0:pad, cols] = qkvbuf[lt:lt + pad, cols]
        dst = slice((blk * SIDE_COLS) % d, (blk * SIDE_COLS) % d + SIDE_COLS)
        if blk * SIDE_COLS < d:
            q_scr[:, dst] = _head_l2norm(cq, HEAD ** -0.5)
        elif blk * SIDE_COLS < 2 * d:
            k_scr[:, dst] = _head_l2norm(cq, 1.0)
        else:
            v_scr[:, dst] = cq

    def proj_qkv(blk):
        cols = slice(blk * SIDE_COLS, (blk + 1) * SIDE_COLS)
        qkvbuf[pad:pad + lt, cols] = jnp.dot(hb, wmain_ref[:, 4 * d + blk * SIDE_COLS:4 * d + (blk + 1) * SIDE_COLS],
                                             preferred_element_type=F32)

    proj_qkv(0)
    for blk in range(dqkv // SIDE_COLS):
        if blk + 1 < dqkv // SIDE_COLS:
            proj_qkv(blk + 1)
        else:
            tick()
        conv_qkv(blk)

    ps = jnp.dot(hb, wsmall_ref[...], preferred_element_type=F32)
    pst = lax.dot_general(wsmallt_ref[...], hb, (((1,), (1,)), ((), ())), preferred_element_type=F32)
    b_scr[...] = _sigmoid(ps)
    g_col = -jnp.exp(alc_ref[...]) * _softplus(ps + dtc_ref[...])
    g_row = -jnp.exp(alr_ref[...]) * _softplus(pst + dtr_ref[...])
    row_in_chunk = lax.broadcasted_iota(jnp.int32, (lt, V7X_LANES), 0) % CHUNK
    gc_col = g_col
    shift = 1
    while shift < CHUNK:
        gc_col = gc_col + jnp.where(row_in_chunk >= shift, pltpu.roll(gc_col, shift, axis=0), 0.0)
        shift *= 2
    gc_scr[...] = gc_col
    gt_scr[...] = jnp.concatenate(
        [jnp.broadcast_to(gc_col[(c + 1) * CHUNK - 1:(c + 1) * CHUNK, :], (CHUNK, V7X_LANES))
         for c in range(n_chunks)], axis=0)
    rj = lax.broadcasted_iota(jnp.int32, (CHUNK, CHUNK), 0)
    cj = lax.broadcasted_iota(jnp.int32, (CHUNK, CHUNK), 1)
    triu = jnp.where(rj <= cj, 1.0, 0.0).astype(F32)
    for c in range(n_chunks):
        gcr_scr[c] = _dot_exact(g_row[:, c * CHUNK:(c + 1) * CHUNK], triu)
    tick()

    causal = rj >= cj
    strict = rj > cj
    eye = jnp.where(rj == cj, 1.0, 0.0).astype(F32)

    nb = n_chunks * N_HEADS
    lane = lax.broadcasted_iota(jnp.int32, (CHUNK, 2 * CHUNK), 1)
    low_half = lane < CHUNK

    for c in range(n_chunks):
        rows = slice(c * CHUNK, (c + 1) * CHUNK)
        beta_blk = b_scr[rows, :]
        gc_blk = gc_scr[rows, :]
        gt_blk = gt_scr[rows, :]
        gcr_blk = gcr_scr[c]
        for hd in range(N_HEADS):
            n = c * N_HEADS + hd
            sl = slice(hd * HEAD, (hd + 1) * HEAD)
            qh = q_scr[rows, sl]
            kh = k_scr[rows, sl]
            vh = v_scr[rows, sl]
            bcol = _lane_bcast(beta_blk, hd, HEAD)
            gcol = _lane_bcast(gc_blk, N_HEADS + hd, HEAD)
            gtot = _lane_bcast(gt_blk, N_HEADS + hd, HEAD)
            grow = gcr_blk[N_HEADS + hd:N_HEADS + hd + 1, :]
            decay = jnp.exp(jnp.where(causal, gcol[:, 0:CHUNK] - grow, -jnp.inf))
            egc = jnp.exp(gcol)
            kb = kh * bcol
            xm = -jnp.where(strict, _dot_nt(kb, kh) * decay, 0.0)
            r_scr[n] = jnp.concatenate([xm, eye], axis=-1)
            aqk_scr[n] = (_dot_nt(qh, kh) * decay).astype(BF16)
            rhs_scr[n] = jnp.concatenate([vh * bcol, kb * egc], axis=-1)
            qd_scr[n] = qh * egc
            kd_scr[n] = (kh * jnp.exp(gtot - gcol)).astype(BF16)
            egt_scr[n] = jnp.exp(gtot[0:V7X_SUBLANES, :])
            if n % SIDE_EVERY_PREP == SIDE_EVERY_PREP - 1:
                tick()

    def split_parts(r):
        hi = r.astype(BF16)
        hi32 = hi.astype(F32)
        lo32 = r - hi32
        return hi, hi32, lo32

    m = 1
    while m < CHUNK:
        for n in range(nb):
            r = r_scr[n]
            hi, hi32, lo32 = split_parts(r)
            lhs = jnp.where(low_half, hi32, pltpu.roll(lo32, CHUNK, axis=1)).astype(BF16)
            lo = lo32.astype(BF16)
            prod = jnp.dot(jnp.concatenate([lhs, lhs], axis=1), jnp.concatenate([hi, hi, lo, lo], axis=0),
                           preferred_element_type=F32)
            r_scr[n] = prod + jnp.where(low_half, 0.0, r)
            if n % SIDE_EVERY == SIDE_EVERY - 1:
                tick()
        m *= 2
    for n in range(nb):
        hi, hi32, lo32 = split_parts(r_scr[n])
        lhs = jnp.where(low_half, pltpu.roll(lo32, CHUNK, axis=1), hi32).astype(BF16)
        bh, _, bl32 = split_parts(rhs_scr[n])
        bl = bl32.astype(BF16)
        sol = jnp.dot(jnp.concatenate([lhs, lhs], axis=1), jnp.concatenate([bh, bh, bl, bl], axis=0),
                      preferred_element_type=F32)
        wu_scr[n] = jnp.concatenate([sol[:, HEAD:2 * HEAD], sol[:, 0:HEAD]], axis=1).astype(BF16)
        if n % SIDE_EVERY == SIDE_EVERY - 1:
            tick()
    for n in range(nb):
        wu = wu_scr[n]
        kw = lax.dot_general(kd_scr[n], wu, (((0,), (0,)), ((), ())), preferred_element_type=F32)
        aw = jnp.dot(aqk_scr[n], wu, preferred_element_type=F32)
        mp_scr[n, 0:HEAD, :] = kw[:, 0:HEAD].astype(BF16)
        mp_scr[n, HEAD:HEAD + CHUNK, :] = (qd_scr[n] - aw[:, 0:HEAD]).astype(BF16)
        sb_scr[n] = kw[:, HEAD:2 * HEAD]
        ob_scr[n] = aw[:, HEAD:2 * HEAD]
        if n % SIDE_EVERY == SIDE_EVERY - 1:
            tick()

    for c in range(n_chunks):
        rows = slice(c * CHUNK, (c + 1) * CHUNK)
        for hd in range(N_HEADS):
            n = c * N_HEADS + hd
            sl = slice(hd * HEAD, (hd + 1) * HEAD)
            s_old = s_scr[hd]
            res = jnp.dot(mp_scr[n], s_old.astype(BF16), preferred_element_type=F32)
            o_scr[rows, sl] = res[HEAD:HEAD + CHUNK] + ob_scr[n]
            s_scr[hd] = (s_old * jnp.concatenate([egt_scr[n]] * (HEAD // V7X_SUBLANES), axis=0)
                         - res[0:HEAD] + sb_scr[n])
        tick()
    while side_work:
        tick()

    on = _head_rms_gate(o_scr[...], onw_ref[...], pr_scr[:, 0:d])
    yb = jnp.dot(on.astype(BF16), wob_ref[...], preferred_element_type=F32)
    merged = _sigmoid(pr_scr[:, d:2 * d]) * ya_scr[...] + _sigmoid(pr_scr[:, 2 * d:3 * d]) * yb
    post = _rms(jnp.dot(merged.astype(BF16), wo_ref[...], preferred_element_type=F32), npost_ref[...])
    y_ref[0] = x + gate_ref[0] * post

    @pl.when(t == pl.num_programs(1) - 1)
    def _():
        s_ref[0] = s_scr[...]


def _prompt(x, shift, scale, gate, consts):
    bsz, seq, d = x.shape
    dqkv = 3 * d
    lt = PROMPT_TILE
    pad = V7X_SUBLANES
    nb = (lt // CHUNK) * N_HEADS
    vm = pl.BlockSpec(memory_space=pltpu.VMEM)
    per_b = pl.BlockSpec((1, 1, d), lambda b, t: (b, 0, 0))
    return pl.pallas_call(
        _prompt_kernel,
        grid=(bsz, seq // lt),
        in_specs=[pl.BlockSpec((1, lt, d), lambda b, t: (b, t, 0)), per_b, per_b, per_b] + [vm] * len(consts),
        out_specs=[pl.BlockSpec((1, lt, d), lambda b, t: (b, t, 0)),
                   pl.BlockSpec((1, CONV_A_W - 1, d), lambda b, t: (b, 0, 0)),
                   pl.BlockSpec((1, CONV_B_W - 1, dqkv), lambda b, t: (b, 0, 0)),
                   pl.BlockSpec((1, N_HEADS, HEAD, HEAD), lambda b, t: (b, 0, 0, 0))],
        out_shape=[jax.ShapeDtypeStruct((bsz, seq, d), F32),
                   jax.ShapeDtypeStruct((bsz, CONV_A_W - 1, d), F32),
                   jax.ShapeDtypeStruct((bsz, CONV_B_W - 1, dqkv), F32),
                   jax.ShapeDtypeStruct((bsz, N_HEADS, HEAD, HEAD), F32)],
        scratch_shapes=[pltpu.VMEM((pad + lt, d), F32),
                        pltpu.VMEM((pad + lt, dqkv), F32),
                        pltpu.VMEM((N_HEADS, HEAD, HEAD), F32),
                        pltpu.VMEM((lt, d), F32), pltpu.VMEM((lt, d), F32),
                        pltpu.VMEM((lt, d), F32), pltpu.VMEM((lt, d), F32),
                        pltpu.VMEM((lt, V7X_LANES), F32), pltpu.VMEM((lt, V7X_LANES), F32),
                        pltpu.VMEM((lt, V7X_LANES), F32),
                        pltpu.VMEM((lt // CHUNK, 2 * N_HEADS, CHUNK), F32),
                        pltpu.VMEM((nb, CHUNK, 2 * CHUNK), F32),
                        pltpu.VMEM((nb, CHUNK, CHUNK), BF16), pltpu.VMEM((nb, CHUNK, 2 * HEAD), F32),
                        pltpu.VMEM((nb, CHUNK, HEAD), F32), pltpu.VMEM((nb, CHUNK, HEAD), BF16),
                        pltpu.VMEM((nb, V7X_SUBLANES, HEAD), F32), pltpu.VMEM((nb, HEAD + CHUNK, HEAD), BF16),
                        pltpu.VMEM((nb, HEAD, HEAD), F32), pltpu.VMEM((nb, CHUNK, HEAD), F32),
                        pltpu.VMEM((nb, CHUNK, 2 * HEAD), BF16),
                        pltpu.VMEM((lt, 4 * d), F32), pltpu.VMEM((lt, 3 * d), F32),
                        pltpu.VMEM((lt, d), BF16), pltpu.VMEM((lt, d), F32)],
        compiler_params=pltpu.CompilerParams(dimension_semantics=("arbitrary", "arbitrary"),
                                             vmem_limit_bytes=V7X_VMEM_LIMIT_BYTES),
        name="prompt",
    )(x, shift, scale, gate, *consts)


def _sample_front_kernel(x_ref, shift_ref, scale_ref, sa_ref, sb_ref, npre_ref, wmain_ref, wgate_ref, wsmall_ref,
                         caw_ref, cbw_ref, alc_ref, dtc_ref, woa_ref,
                         q_ref, k_ref, v_ref, be_ref, eg_ref, ya_ref, rest_ref, nsa_ref, nsb_ref):
    d = x_ref.shape[1]
    dqkv = 3 * d
    n = x_ref.shape[0]
    x = x_ref[...]
    h = _rms(x, npre_ref[...]) * (1.0 + scale_ref[...]) + shift_ref[...]
    hb = h.astype(BF16)

    pa = jnp.dot(hb, wmain_ref[:, 0:4 * d], preferred_element_type=F32)
    u = pa[:, d:2 * d] * pa[:, 0:d]
    conv = u * caw_ref[CONV_A_W - 1:CONV_A_W, :]
    for j in range(CONV_A_W - 1):
        conv = conv + sa_ref[:, j * d:(j + 1) * d] * caw_ref[j:j + 1, :]
    ya_ref[...] = jnp.dot(((pa[:, 2 * d:3 * d] * conv) * _silu(pa[:, 3 * d:4 * d])).astype(BF16), woa_ref[...],
                          preferred_element_type=F32)
    nsa_ref[:, 0:(CONV_A_W - 2) * d] = sa_ref[:, d:(CONV_A_W - 1) * d]
    nsa_ref[:, (CONV_A_W - 2) * d:(CONV_A_W - 1) * d] = u

    pq = jnp.dot(hb, wmain_ref[:, 4 * d:4 * d + dqkv], preferred_element_type=F32)
    cq = pq * cbw_ref[CONV_B_W - 1:CONV_B_W, :]
    for j in range(CONV_B_W - 1):
        cq = cq + sb_ref[:, j * dqkv:(j + 1) * dqkv] * cbw_ref[j:j + 1, :]
    cq = _silu(cq)
    nsb_ref[:, 0:(CONV_B_W - 2) * dqkv] = sb_ref[:, dqkv:(CONV_B_W - 1) * dqkv]
    nsb_ref[:, (CONV_B_W - 2) * dqkv:(CONV_B_W - 1) * dqkv] = pq
    q_ref[...] = _head_l2norm(cq[:, 0:d], HEAD ** -0.5)
    k_ref[...] = _head_l2norm(cq[:, d:2 * d], 1.0)
    v_ref[...] = cq[:, 2 * d:3 * d]

    rest_ref[:, 0:d] = jnp.dot(hb, wmain_ref[:, 4 * d + dqkv:5 * d + dqkv], preferred_element_type=F32)
    rest_ref[:, d:3 * d] = jnp.dot(hb, wgate_ref[...], preferred_element_type=F32)

    ps = jnp.dot(hb, wsmall_ref[...], preferred_element_type=F32)
    beta = _sigmoid(ps)
    eg = jnp.exp(-jnp.exp(alc_ref[...]) * _softplus(ps + dtc_ref[...]))
    be_ref[...] = jnp.concatenate([_lane_bcast(beta, hd, HEAD) for hd in range(N_HEADS)], axis=-1)
    eg_ref[...] = jnp.concatenate([_lane_bcast(eg, N_HEADS + hd, HEAD) for hd in range(N_HEADS)], axis=-1)


def _sample_state_kernel(s_ref, q_ref, k_ref, v_ref, be_ref, eg_ref, so_ref, o_ref):
    bb = q_ref.shape[0]
    row = lax.broadcasted_iota(jnp.int32, (bb, HEAD), 0)
    erow = lax.broadcasted_iota(jnp.int32, (bb, bb * HEAD), 0)
    elane = lax.broadcasted_iota(jnp.int32, (bb, bb * HEAD), 1)
    diag = (elane // HEAD) == erow
    zeros_pad = jnp.zeros((2 * V7X_SUBLANES - bb, HEAD), F32)
    for hd in range(N_HEADS):
        sl = slice(hd * HEAD, (hd + 1) * HEAD)
        qh = q_ref[:, sl]
        kh = k_ref[:, sl]
        vh = v_ref[:, sl]
        eg = eg_ref[:, sl]
        kq = jnp.concatenate([kh, qh], axis=0).astype(BF16)
        ks = jnp.zeros((bb, HEAD), F32)
        qs = jnp.zeros((bb, HEAD), F32)
        for tk in range(bb):
            r = jnp.dot(kq, s_ref[tk, hd].astype(BF16), preferred_element_type=F32)
            ks = jnp.where(row == tk, r[0:bb], ks)
            qs = jnp.where(row == tk, r[bb:2 * bb], qs)
        vn = be_ref[:, sl] * (vh - eg * ks)
        o_ref[:, sl] = eg * qs + jnp.sum(qh * kh, axis=-1, keepdims=True) * vn
        vexp = jnp.where(diag, jnp.concatenate([vn] * bb, axis=-1), 0.0)
        outer = _dot_tn(jnp.concatenate([kh, zeros_pad], axis=0),
                        jnp.concatenate([vexp, jnp.zeros((2 * V7X_SUBLANES - bb, bb * HEAD), F32)], axis=0))
        for tk in range(bb):
            so_ref[tk, hd] = (s_ref[tk, hd] * jnp.broadcast_to(eg[tk:tk + 1, :], (HEAD, HEAD))
                              + outer[:, tk * HEAD:(tk + 1) * HEAD])


def _sample_back_kernel(x_ref, gate_ref, o_ref, ya_ref, rest_ref, onw_ref, wob_ref, wo_ref, npost_ref, y_ref):
    d = x_ref.shape[1]
    on = _head_rms_gate(o_ref[...], onw_ref[...], rest_ref[:, 0:d])
    yb = jnp.dot(on.astype(BF16), wob_ref[...], preferred_element_type=F32)
    merged = _sigmoid(rest_ref[:, d:2 * d]) * ya_ref[...] + _sigmoid(rest_ref[:, 2 * d:3 * d]) * yb
    post = _rms(jnp.dot(merged.astype(BF16), wo_ref[...], preferred_element_type=F32), npost_ref[...])
    y_ref[...] = x_ref[...] + gate_ref[...] * post


def _sample(x, shift, scale, gate, sa, sb, s0, w):
    n, d = x.shape
    dqkv = 3 * d
    vm = pl.BlockSpec(memory_space=pltpu.VMEM)
    params = pltpu.CompilerParams(vmem_limit_bytes=V7X_VMEM_LIMIT_BYTES)
    row = jax.ShapeDtypeStruct((n, d), F32)
    front_in = (x, shift, scale, sa, sb, w["npre"], w["wmain"], w["wgate"], w["wsmall"], w["caw"], w["cbw"],
                w["alc"], w["dtc"], w["woa"])
    q, k, v, be, eg, ya, rest, nsa, nsb = pl.pallas_call(
        _sample_front_kernel,
        in_specs=[vm] * len(front_in),
        out_specs=[vm] * 9,
        out_shape=[row, row, row, row, row, row, jax.ShapeDtypeStruct((n, 3 * d), F32),
                   jax.ShapeDtypeStruct(sa.shape, F32), jax.ShapeDtypeStruct(sb.shape, F32)],
        compiler_params=params,
        name="sample_front",
    )(*front_in)

    bb = SAMPLE_BLOCK
    tok = pl.BlockSpec((bb, d), lambda i: (i, 0))
    st = pl.BlockSpec((bb, N_HEADS, HEAD, HEAD), lambda i: (i, 0, 0, 0))
    s_new, o = pl.pallas_call(
        _sample_state_kernel,
        grid=(n // bb,),
        in_specs=[st, tok, tok, tok, tok, tok],
        out_specs=[st, tok],
        out_shape=[jax.ShapeDtypeStruct(s0.shape, F32), row],
        compiler_params=pltpu.CompilerParams(dimension_semantics=("arbitrary",),
                                             vmem_limit_bytes=V7X_VMEM_LIMIT_BYTES),
        name="sample_state",
    )(s0, q, k, v, be, eg)

    back_in = (x, gate, o, ya, rest, w["onw"], w["wob"], w["wo"], w["npost"])
    y = pl.pallas_call(
        _sample_back_kernel,
        in_specs=[vm] * len(back_in),
        out_specs=vm,
        out_shape=row,
        compiler_params=params,
        name="sample_back",
    )(*back_in)
    return y, nsa, nsb, s_new


def kernel(x_prompt, x_sample, c_prompt, c_sample, state_conv_a, state_conv_qkv, state_delta, ada_w, ada_b,
           norm_pre, w_in, conv_a_w, conv_b_w, a_log, dt_bias, onorm_w, w_out_a, w_out_b, w_o, norm_post):
    depth = w_in.shape[0]
    assert depth == 1, "single-layer trunk"
    bp, seq, d = x_prompt.shape
    ns = x_sample.shape[0]
    dqkv = 3 * d
    assert d == N_HEADS * HEAD and seq % PROMPT_TILE == 0 and PROMPT_TILE % CHUNK == 0 and 2 * CHUNK == V7X_LANES
    assert ns % SAMPLE_BLOCK == 0 and x_sample.shape[1] == 1

    off_small = 4 * d + dqkv + d
    w0 = w_in[0]
    wmain = w0[:, :off_small].astype(BF16)
    wgate = w0[:, off_small + 2 * N_HEADS:].astype(BF16)
    wsmall_f = jnp.pad(w0[:, off_small:off_small + 2 * N_HEADS], ((0, 0), (0, V7X_LANES - 2 * N_HEADS)))
    wsmall = wsmall_f.astype(BF16)
    wsmallt = w0[:, off_small:off_small + 2 * N_HEADS].T.astype(BF16)
    zeros_h = jnp.zeros((N_HEADS,), F32)
    lane_pad = jnp.zeros((V7X_LANES - 2 * N_HEADS,), F32)
    w = {
        "npre": norm_pre, "npost": norm_post, "onw": onorm_w,
        "wmain": wmain, "wgate": wgate, "wsmall": wsmall, "wsmallt": wsmallt,
        "caw": conv_a_w[0], "cbw": conv_b_w[0],
        "alc": jnp.concatenate([zeros_h, a_log[0], lane_pad])[None, :],
        "dtc": jnp.concatenate([zeros_h, dt_bias[0], lane_pad])[None, :],
        "alr": jnp.concatenate([zeros_h, a_log[0]])[:, None],
        "dtr": jnp.concatenate([zeros_h, dt_bias[0]])[:, None],
        "woa": w_out_a[0].astype(BF16), "wob": w_out_b[0].astype(BF16), "wo": w_o[0].astype(BF16),
    }

    mod = _adaln(jnp.concatenate([c_prompt, c_sample], axis=0), ada_w[0], ada_b)
    shift, scale, gate = mod[:, 0:d], mod[:, d:2 * d], mod[:, 2 * d:3 * d]

    consts = (w["npre"], w["wmain"], w["wgate"], w["wsmall"], w["wsmallt"], w["caw"], w["cbw"], w["alc"], w["dtc"],
              w["alr"], w["dtr"], w["onw"], w["woa"], w["wob"], w["wo"], w["npost"])
    yp, pca, pcb, pds = _prompt(x_prompt, shift[:bp, None, :], scale[:bp, None, :], gate[:bp, None, :], consts)

    ys, nsa, nsb, sds = _sample(
        x_sample[:, 0, :], shift[bp:], scale[bp:], gate[bp:],
        state_conv_a[0].reshape(ns, (CONV_A_W - 1) * d), state_conv_qkv[0].reshape(ns, (CONV_B_W - 1) * dqkv),
        state_delta[0], w)

    return (yp, ys[:, None, :], pca[None], pcb[None], pds[None].astype(state_delta.dtype),
            nsa.reshape(1, ns, CONV_A_W - 1, d), nsb.reshape(1, ns, CONV_B_W - 1, dqkv),
            sds[None].astype(state_delta.dtype))
```

```python
import functools

import jax
import jax.numpy as jnp
from jax import lax
from jax.experimental import pallas as pl
from jax.experimental.pallas import tpu as pltpu

F32 = jnp.float32
BF16 = jnp.bfloat16

EPS = 1e-6
N_HEADS = 8
HEAD = 128
CONV_A_W = 3
CONV_B_W = 4

V7X_LANES = 128
V7X_SUBLANES = 8
V7X_VMEM_LIMIT_BYTES = 60 * 1024 * 1024

PROMPT_TILE = 256
SIDE_COLS = 512
SIDE_EVERY_PREP = 4
SIDE_EVERY_SOLVE = 32
SIDE_EVERY = 16
CHUNK = 64
SAMPLE_BLOCK = 8

HI = lax.Precision.HIGHEST


NEG_LOG2_E = -1.4426950408889634


def _sigmoid(x):
    return 1.0 / (1.0 + jnp.exp2(x * NEG_LOG2_E))


def _silu(x):
    return x * _sigmoid(x)


def _softplus(x):
    return jnp.maximum(x, 0.0) + jnp.log1p(jnp.exp(-jnp.abs(x)))


def _dot(a, b):
    return jnp.dot(a.astype(BF16), b.astype(BF16), preferred_element_type=F32)


def _dot_nt(a, b):
    return lax.dot_general(a.astype(BF16), b.astype(BF16), (((1,), (1,)), ((), ())),
                           preferred_element_type=F32)


def _dot_tn(a, b):
    return lax.dot_general(a.astype(BF16), b.astype(BF16), (((0,), (0,)), ((), ())),
                           preferred_element_type=F32)


def _dot_exact(a, b):
    return jnp.dot(a, b, precision=HI, preferred_element_type=F32)


def _split(a):
    hi = a.astype(BF16)
    return hi, (a - hi.astype(F32)).astype(BF16)


def _solve_dot(a, b):
    ah, al = _split(a)
    bh, bl = _split(b)
    return (jnp.dot(ah, bh, preferred_element_type=F32) + jnp.dot(ah, bl, preferred_element_type=F32)
            + jnp.dot(al, bh, preferred_element_type=F32))


def _rms(x, w):
    return x * lax.rsqrt(jnp.mean(x * x, axis=-1, keepdims=True) + EPS) * w


def _head_l2norm(x, scale):
    outs = []
    for h in range(x.shape[1] // HEAD):
        xh = x[:, h * HEAD:(h + 1) * HEAD]
        outs.append(xh * (lax.rsqrt(jnp.sum(xh * xh, axis=-1, keepdims=True) + EPS) * scale))
    return jnp.concatenate(outs, axis=-1)


def _head_rms_gate(o, w, z):
    outs = []
    for h in range(N_HEADS):
        sl = slice(h * HEAD, (h + 1) * HEAD)
        oh = o[:, sl]
        outs.append(oh * lax.rsqrt(jnp.mean(oh * oh, axis=-1, keepdims=True) + EPS) * w * _silu(z[:, sl]))
    return jnp.concatenate(outs, axis=-1)


def _lane_bcast(x, col, width):
    return jnp.broadcast_to(x[:, col:col + 1], (x.shape[0], width))


def _adaln_kernel(c_ref, w_ref, b_ref, o_ref):
    o_ref[...] = _dot(_silu(c_ref[...]), w_ref[...]) + b_ref[...]


def _adaln(c_all, ada_w, ada_b):
    n, d = c_all.shape
    cols = ada_w.shape[1]
    return pl.pallas_call(
        _adaln_kernel,
        grid=(cols // d,),
        in_specs=[pl.BlockSpec((n, d), lambda j: (0, 0)),
                  pl.BlockSpec((d, d), lambda j: (0, j)),
                  pl.BlockSpec((1, d), lambda j: (0, j))],
        out_specs=pl.BlockSpec((n, d), lambda j: (0, j)),
        out_shape=jax.ShapeDtypeStruct((n, cols), F32),
        name="adaln",
    )(c_all, ada_w, ada_b)


def _prompt_kernel(x_ref, shift_ref, scale_ref, gate_ref, npre_ref, wmain_ref, wgate_ref, wsmall_ref, wsmallt_ref,
                   caw_ref, cbw_ref, alc_ref, dtc_ref, alr_ref, dtr_ref, onw_ref, woa_ref, wob_ref, wo_ref,
                   npost_ref,
                   y_ref, ca_ref, cb_ref, s_ref,
                   abuf, qkvbuf, s_scr, q_scr, k_scr, v_scr, o_scr, b_scr, gc_scr, gt_scr, gcr_scr,
                   r_scr, aqk_scr, rhs_scr, qd_scr, kd_scr, egt_scr, mp_scr, sb_scr, ob_scr, wu_scr,
                   pa_scr, pr_scr, yap_scr, ya_scr):
    lt = x_ref.shape[1]
    d = x_ref.shape[2]
    dqkv = qkvbuf.shape[1]
    n_chunks = lt // CHUNK
    pad = V7X_SUBLANES
    t = pl.program_id(1)

    @pl.when(t == 0)
    def _():
        abuf[0:pad, :] = jnp.zeros((pad, d), F32)
        qkvbuf[0:pad, :] = jnp.zeros((pad, dqkv), F32)
        s_scr[...] = jnp.zeros(s_scr.shape, F32)

    x = x_ref[0]
    h = _rms(x, npre_ref[...]) * (1.0 + scale_ref[0]) + shift_ref[0]
    hb = h.astype(BF16)

    side_work = []

    def side_proj(dst_ref, dst_off, w_ref, w_off, blk):
        def run():
            dst_ref[:, dst_off + blk * SIDE_COLS:dst_off + (blk + 1) * SIDE_COLS] = jnp.dot(
                hb, w_ref[:, w_off + blk * SIDE_COLS:w_off + (blk + 1) * SIDE_COLS], preferred_element_type=F32)
        return run

    def side_branch_a(blk):
        def run():
            cols = slice(blk * SIDE_COLS, (blk + 1) * SIDE_COLS)
            u = pa_scr[:, d + blk * SIDE_COLS:d + (blk + 1) * SIDE_COLS] * pa_scr[:, cols]
            abuf[pad:pad + lt, cols] = u
            ext = abuf[:, cols]
            conv = u * caw_ref[CONV_A_W - 1:CONV_A_W, cols]
            for j in range(CONV_A_W - 1):
                conv = conv + pltpu.roll(ext, CONV_A_W - 1 - j, axis=0)[pad:pad + lt] * caw_ref[j:j + 1, cols]
            b_a = pa_scr[:, 2 * d + blk * SIDE_COLS:2 * d + (blk + 1) * SIDE_COLS]
            z_a = pa_scr[:, 3 * d + blk * SIDE_COLS:3 * d + (blk + 1) * SIDE_COLS]
            yap_scr[:, cols] = ((b_a * conv) * _silu(z_a)).astype(BF16)
            ca_ref[0, :, cols] = abuf[pad + lt - (CONV_A_W - 1):pad + lt, cols]
            abuf[0:pad, cols] = abuf[lt:lt + pad, cols]
        return run

    def side_out_a(blk):
        def run():
            cols = slice(blk * SIDE_COLS, (blk + 1) * SIDE_COLS)
            ya_scr[:, cols] = jnp.dot(yap_scr[...], woa_ref[:, cols], preferred_element_type=F32)
        return run

    side_work += [side_proj(pa_scr, 0, wmain_ref, 0, blk) for blk in range(4 * d // SIDE_COLS)]
    side_work += [side_proj(pr_scr, 0, wmain_ref, 4 * d + dqkv, blk) for blk in range(d // SIDE_COLS)]
    side_work += [side_proj(pr_scr, d, wgate_ref, 0, blk) for blk in range(2 * d // SIDE_COLS)]
    side_work += [side_branch_a(blk) for blk in range(d // SIDE_COLS)]
    side_work += [side_out_a(blk) for blk in range(d // SIDE_COLS)]
    side_work.reverse()

    def tick():
        if side_work:
            side_work.pop()()

    def conv_qkv(blk):
        cols = slice(blk * SIDE_COLS, (blk + 1) * SIDE_COLS)
        ext = qkvbuf[:, cols]
        cq = ext[pad:pad + lt] * cbw_ref[CONV_B_W - 1:CONV_B_W, cols]
        for j in range(CONV_B_W - 1):
            cq = cq + pltpu.roll(ext, CONV_B_W - 1 - j, axis=0)[pad:pad + lt] * cbw_ref[j:j + 1, cols]
        cq = _silu(cq)
        cb_ref[0, :, cols] = qkvbuf[pad + lt - (CONV_B_W - 1):pad + lt, cols]
        qkvbuf[0:pad, cols] = qkvbuf[lt:lt + pad, cols]
        dst = slice((blk * SIDE_COLS) % d, (blk * SIDE_COLS) % d + SIDE_COLS)
        if blk * SIDE_COLS < d:
            q_scr[:, dst] = _head_l2norm(cq, HEAD ** -0.5)
        elif blk * SIDE_COLS < 2 * d:
            k_scr[:, dst] = _head_l2norm(cq, 1.0)
        else:
            v_scr[:, dst] = cq

    def proj_qkv(blk):
        cols = slice(blk * SIDE_COLS, (blk + 1) * SIDE_COLS)
        qkvbuf[pad:pad + lt, cols] = jnp.dot(hb, wmain_ref[:, 4 * d + blk * SIDE_COLS:4 * d + (blk + 1) * SIDE_COLS],
                                             preferred_element_type=F32)

    proj_qkv(0)
    for blk in range(dqkv // SIDE_COLS):
        if blk + 1 < dqkv // SIDE_COLS:
            proj_qkv(blk + 1)
        else:
            tick()
        conv_qkv(blk)

    ps = jnp.dot(hb, wsmall_ref[...], preferred_element_type=F32)
    pst = lax.dot_general(wsmallt_ref[...], hb, (((1,), (1,)), ((), ())), preferred_element_type=F32)
    b_scr[...] = _sigmoid(ps)
    g_col = -jnp.exp(alc_ref[...]) * _softplus(ps + dtc_ref[...])
    g_row = -jnp.exp(alr_ref[...]) * _softplus(pst + dtr_ref[...])
    row_in_chunk = lax.broadcasted_iota(jnp.int32, (lt, V7X_LANES), 0) % CHUNK
    gc_col = g_col
    shift = 1
    while shift < CHUNK:
        gc_col = gc_col + jnp.where(row_in_chunk >= shift, pltpu.roll(gc_col, shift, axis=0), 0.0)
        shift *= 2
    gc_scr[...] = gc_col
    gt_scr[...] = jnp.concatenate(
        [jnp.broadcast_to(gc_col[(c + 1) * CHUNK - 1:(c + 1) * CHUNK, :], (CHUNK, V7X_LANES))
         for c in range(n_chunks)], axis=0)
    rj = lax.broadcasted_iota(jnp.int32, (CHUNK, CHUNK), 0)
    cj = lax.broadcasted_iota(jnp.int32, (CHUNK, CHUNK), 1)
    triu = jnp.where(rj <= cj, 1.0, 0.0).astype(F32)
    for c in range(n_chunks):
        gcr_scr[c] = _dot_exact(g_row[:, c * CHUNK:(c + 1) * CHUNK], triu)
    tick()

    causal = rj >= cj
    strict = rj > cj
    eye = jnp.where(rj == cj, 1.0, 0.0).astype(F32)

    nb = n_chunks * N_HEADS
    lane = lax.broadcasted_iota(jnp.int32, (CHUNK, 2 * CHUNK), 1)
    low_half = lane < CHUNK
    eye_left = jnp.where(lane == lax.broadcasted_iota(jnp.int32, (CHUNK, 2 * CHUNK), 0), 1.0, 0.0).astype(BF16)

    for c in range(n_chunks):
        rows = slice(c * CHUNK, (c + 1) * CHUNK)
        beta_blk = b_scr[rows, :]
        gc_blk = gc_scr[rows, :]
        gt_blk = gt_scr[rows, :]
        gcr_blk = gcr_scr[c]
        for hd in range(N_HEADS):
            n = c * N_HEADS + hd
            sl = slice(hd * HEAD, (hd + 1) * HEAD)
            qh = q_scr[rows, sl]
            kh = k_scr[rows, sl]
            vh = v_scr[rows, sl]
            bcol = _lane_bcast(beta_blk, hd, HEAD)
            gcol = _lane_bcast(gc_blk, N_HEADS + hd, HEAD)
            gtot = _lane_bcast(gt_blk, N_HEADS + hd, HEAD)
            grow = gcr_blk[N_HEADS + hd:N_HEADS + hd + 1, :]
            decay = jnp.exp(jnp.where(causal, gcol[:, 0:CHUNK] - grow, -jnp.inf))
            egc = jnp.exp(gcol)
            kb = kh * bcol
            xm = -jnp.where(strict, _dot_nt(kb, kh) * decay, 0.0)
            r_scr[n] = jnp.concatenate([xm, eye], axis=-1)
            aqk_scr[n] = (_dot_nt(qh, kh) * decay).astype(BF16)
            rhs_scr[n] = jnp.concatenate([vh * bcol, kb * egc], axis=-1)
            qd_scr[n] = qh * egc
            kd_scr[n] = (kh * jnp.exp(gtot - gcol)).astype(BF16)
            egt_scr[n] = jnp.exp(gtot[0:V7X_SUBLANES, :])
            if n % SIDE_EVERY_PREP == SIDE_EVERY_PREP - 1:
                tick()

    def split_parts(r):
        hi = r.astype(BF16)
        hi32 = hi.astype(F32)
        lo32 = r - hi32
        return hi, hi32, lo32

    m = 1
    while m < CHUNK:
        for n in range(nb):
            r = r_scr[n]
            hi, hi32, lo32 = split_parts(r)
            lhs = jnp.where(low_half, hi32, pltpu.roll(lo32, CHUNK, axis=1)).astype(BF16)
            lo = lo32.astype(BF16)
            him = hi - eye_left
            prod = jnp.dot(jnp.concatenate([lhs, lhs], axis=1), jnp.concatenate([him, him, lo, lo], axis=0),
                           preferred_element_type=F32)
            r_scr[n] = prod + r
            if n % SIDE_EVERY_SOLVE == SIDE_EVERY_SOLVE - 1:
                tick()
        m *= 2
    for n in range(nb):
        hi, hi32, lo32 = split_parts(r_scr[n])
        lhs = jnp.where(low_half, pltpu.roll(lo32, CHUNK, axis=1), hi32).astype(BF16)
        bh, _, bl32 = split_parts(rhs_scr[n])
        bl = bl32.astype(BF16)
        sol = jnp.dot(jnp.concatenate([lhs, lhs], axis=1), jnp.concatenate([bh, bh, bl, bl], axis=0),
                      preferred_element_type=F32)
        wu_scr[n] = jnp.concatenate([sol[:, HEAD:2 * HEAD], sol[:, 0:HEAD]], axis=1).astype(BF16)
        if n % SIDE_EVERY == SIDE_EVERY - 1:
            tick()
    for n in range(nb):
        wu = wu_scr[n]
        kw = lax.dot_general(kd_scr[n], wu, (((0,), (0,)), ((), ())), preferred_element_type=F32)
        aw = jnp.dot(aqk_scr[n], wu, preferred_element_type=F32)
        mp_scr[n, 0:HEAD, :] = kw[:, 0:HEAD].astype(BF16)
        mp_scr[n, HEAD:HEAD + CHUNK, :] = (qd_scr[n] - aw[:, 0:HEAD]).astype(BF16)
        sb_scr[n] = kw[:, HEAD:2 * HEAD]
        ob_scr[n] = aw[:, HEAD:2 * HEAD]
        if n % SIDE_EVERY == SIDE_EVERY - 1:
            tick()

    for c in range(n_chunks):
        rows = slice(c * CHUNK, (c + 1) * CHUNK)
        for hd in range(N_HEADS):
            n = c * N_HEADS + hd
            sl = slice(hd * HEAD, (hd + 1) * HEAD)
            s_old = s_scr[hd]
            res = jnp.dot(mp_scr[n], s_old.astype(BF16), preferred_element_type=F32)
            o_scr[rows, sl] = res[HEAD:HEAD + CHUNK] + ob_scr[n]
            s_scr[hd] = (s_old * jnp.concatenate([egt_scr[n]] * (HEAD // V7X_SUBLANES), axis=0)
                         - res[0:HEAD] + sb_scr[n])
        tick()
    while side_work:
        tick()

    on = _head_rms_gate(o_scr[...], onw_ref[...], pr_scr[:, 0:d])
    yb = jnp.dot(on.astype(BF16), wob_ref[...], preferred_element_type=F32)
    merged = _sigmoid(pr_scr[:, d:2 * d]) * ya_scr[...] + _sigmoid(pr_scr[:, 2 * d:3 * d]) * yb
    post = _rms(jnp.dot(merged.astype(BF16), wo_ref[...], preferred_element_type=F32), npost_ref[...])
    y_ref[0] = x + gate_ref[0] * post

    @pl.when(t == pl.num_programs(1) - 1)
    def _():
        s_ref[0] = s_scr[...]


def _prompt(x, shift, scale, gate, consts):
    bsz, seq, d = x.shape
    dqkv = 3 * d
    lt = PROMPT_TILE
    pad = V7X_SUBLANES
    nb = (lt // CHUNK) * N_HEADS
    vm = pl.BlockSpec(memory_space=pltpu.VMEM)
    per_b = pl.BlockSpec((1, 1, d), lambda b, t: (b, 0, 0))
    return pl.pallas_call(
        _prompt_kernel,
        grid=(bsz, seq // lt),
        in_specs=[pl.BlockSpec((1, lt, d), lambda b, t: (b, t, 0)), per_b, per_b, per_b] + [vm] * len(consts),
        out_specs=[pl.BlockSpec((1, lt, d), lambda b, t: (b, t, 0)),
                   pl.BlockSpec((1, CONV_A_W - 1, d), lambda b, t: (b, 0, 0)),
                   pl.BlockSpec((1, CONV_B_W - 1, dqkv), lambda b, t: (b, 0, 0)),
                   pl.BlockSpec((1, N_HEADS, HEAD, HEAD), lambda b, t: (b, 0, 0, 0))],
        out_shape=[jax.ShapeDtypeStruct((bsz, seq, d), F32),
                   jax.ShapeDtypeStruct((bsz, CONV_A_W - 1, d), F32),
                   jax.ShapeDtypeStruct((bsz, CONV_B_W - 1, dqkv), F32),
                   jax.ShapeDtypeStruct((bsz, N_HEADS, HEAD, HEAD), F32)],
        scratch_shapes=[pltpu.VMEM((pad + lt, d), F32),
                        pltpu.VMEM((pad + lt, dqkv), F32),
                        pltpu.VMEM((N_HEADS, HEAD, HEAD), F32),
                        pltpu.VMEM((lt, d), F32), pltpu.VMEM((lt, d), F32),
                        pltpu.VMEM((lt, d), F32), pltpu.VMEM((lt, d), F32),
                        pltpu.VMEM((lt, V7X_LANES), F32), pltpu.VMEM((lt, V7X_LANES), F32),
                        pltpu.VMEM((lt, V7X_LANES), F32),
                        pltpu.VMEM((lt // CHUNK, 2 * N_HEADS, CHUNK), F32),
                        pltpu.VMEM((nb, CHUNK, 2 * CHUNK), F32),
                        pltpu.VMEM((nb, CHUNK, CHUNK), BF16), pltpu.VMEM((nb, CHUNK, 2 * HEAD), F32),
                        pltpu.VMEM((nb, CHUNK, HEAD), F32), pltpu.VMEM((nb, CHUNK, HEAD), BF16),
                        pltpu.VMEM((nb, V7X_SUBLANES, HEAD), F32), pltpu.VMEM((nb, HEAD + CHUNK, HEAD), BF16),
                        pltpu.VMEM((nb, HEAD, HEAD), F32), pltpu.VMEM((nb, CHUNK, HEAD), F32),
                        pltpu.VMEM((nb, CHUNK, 2 * HEAD), BF16),
                        pltpu.VMEM((lt, 4 * d), F32), pltpu.VMEM((lt, 3 * d), F32),
                        pltpu.VMEM((lt, d), BF16), pltpu.VMEM((lt, d), F32)],
        compiler_params=pltpu.CompilerParams(dimension_semantics=("arbitrary", "arbitrary"),
                                             vmem_limit_bytes=V7X_VMEM_LIMIT_BYTES),
        name="prompt",
    )(x, shift, scale, gate, *consts)


def _sample_front_kernel(x_ref, shift_ref, scale_ref, sa_ref, sb_ref, npre_ref, wmain_ref, wgate_ref, wsmall_ref,
                         caw_ref, cbw_ref, alc_ref, dtc_ref, woa_ref,
                         q_ref, k_ref, v_ref, be_ref, eg_ref, ya_ref, rest_ref, nsa_ref, nsb_ref):
    d = x_ref.shape[1]
    dqkv = 3 * d
    n = x_ref.shape[0]
    x = x_ref[...]
    h = _rms(x, npre_ref[...]) * (1.0 + scale_ref[...]) + shift_ref[...]
    hb = h.astype(BF16)

    pa = jnp.dot(hb, wmain_ref[:, 0:4 * d], preferred_element_type=F32)
    u = pa[:, d:2 * d] * pa[:, 0:d]
    conv = u * caw_ref[CONV_A_W - 1:CONV_A_W, :]
    for j in range(CONV_A_W - 1):
        conv = conv + sa_ref[:, j * d:(j + 1) * d] * caw_ref[j:j + 1, :]
    ya_ref[...] = jnp.dot(((pa[:, 2 * d:3 * d] * conv) * _silu(pa[:, 3 * d:4 * d])).astype(BF16), woa_ref[...],
                          preferred_element_type=F32)
    nsa_ref[:, 0:(CONV_A_W - 2) * d] = sa_ref[:, d:(CONV_A_W - 1) * d]
    nsa_ref[:, (CONV_A_W - 2) * d:(CONV_A_W - 1) * d] = u

    pq = jnp.dot(hb, wmain_ref[:, 4 * d:4 * d + dqkv], preferred_element_type=F32)
    cq = pq * cbw_ref[CONV_B_W - 1:CONV_B_W, :]
    for j in range(CONV_B_W - 1):
        cq = cq + sb_ref[:, j * dqkv:(j + 1) * dqkv] * cbw_ref[j:j + 1, :]
    cq = _silu(cq)
    nsb_ref[:, 0:(CONV_B_W - 2) * dqkv] = sb_ref[:, dqkv:(CONV_B_W - 1) * dqkv]
    nsb_ref[:, (CONV_B_W - 2) * dqkv:(CONV_B_W - 1) * dqkv] = pq
    q_ref[...] = _head_l2norm(cq[:, 0:d], HEAD ** -0.5)
    k_ref[...] = _head_l2norm(cq[:, d:2 * d], 1.0)
    v_ref[...] = cq[:, 2 * d:3 * d]

    rest_ref[:, 0:d] = jnp.dot(hb, wmain_ref[:, 4 * d + dqkv:5 * d + dqkv], preferred_element_type=F32)
    rest_ref[:, d:3 * d] = jnp.dot(hb, wgate_ref[...], preferred_element_type=F32)

    ps = jnp.dot(hb, wsmall_ref[...], preferred_element_type=F32)
    beta = _sigmoid(ps)
    eg = jnp.exp(-jnp.exp(alc_ref[...]) * _softplus(ps + dtc_ref[...]))
    be_ref[...] = jnp.concatenate([_lane_bcast(beta, hd, HEAD) for hd in range(N_HEADS)], axis=-1)
    eg_ref[...] = jnp.concatenate([_lane_bcast(eg, N_HEADS + hd, HEAD) for hd in range(N_HEADS)], axis=-1)


def _sample_state_kernel(s_ref, q_ref, k_ref, v_ref, be_ref, eg_ref, so_ref, o_ref):
    bb = q_ref.shape[0]
    row = lax.broadcasted_iota(jnp.int32, (bb, HEAD), 0)
    erow = lax.broadcasted_iota(jnp.int32, (bb, bb * HEAD), 0)
    elane = lax.broadcasted_iota(jnp.int32, (bb, bb * HEAD), 1)
    diag = (elane // HEAD) == erow
    zeros_pad = jnp.zeros((2 * V7X_SUBLANES - bb, HEAD), F32)
    for hd in range(N_HEADS):
        sl = slice(hd * HEAD, (hd + 1) * HEAD)
        qh = q_ref[:, sl]
        kh = k_ref[:, sl]
        vh = v_ref[:, sl]
        eg = eg_ref[:, sl]
        kq = jnp.concatenate([kh, qh], axis=0).astype(BF16)
        ks = jnp.zeros((bb, HEAD), F32)
        qs = jnp.zeros((bb, HEAD), F32)
        for tk in range(bb):
            r = jnp.dot(kq, s_ref[tk, hd].astype(BF16), preferred_element_type=F32)
            ks = jnp.where(row == tk, r[0:bb], ks)
            qs = jnp.where(row == tk, r[bb:2 * bb], qs)
        vn = be_ref[:, sl] * (vh - eg * ks)
        o_ref[:, sl] = eg * qs + jnp.sum(qh * kh, axis=-1, keepdims=True) * vn
        vexp = jnp.where(diag, jnp.concatenate([vn] * bb, axis=-1), 0.0)
        outer = _dot_tn(jnp.concatenate([kh, zeros_pad], axis=0),
                        jnp.concatenate([vexp, jnp.zeros((2 * V7X_SUBLANES - bb, bb * HEAD), F32)], axis=0))
        for tk in range(bb):
            so_ref[tk, hd] = (s_ref[tk, hd] * jnp.broadcast_to(eg[tk:tk + 1, :], (HEAD, HEAD))
                              + outer[:, tk * HEAD:(tk + 1) * HEAD])


def _sample_back_kernel(x_ref, gate_ref, o_ref, ya_ref, rest_ref, onw_ref, wob_ref, wo_ref, npost_ref, y_ref):
    d = x_ref.shape[1]
    on = _head_rms_gate(o_ref[...], onw_ref[...], rest_ref[:, 0:d])
    yb = jnp.dot(on.astype(BF16), wob_ref[...], preferred_element_type=F32)
    merged = _sigmoid(rest_ref[:, d:2 * d]) * ya_ref[...] + _sigmoid(rest_ref[:, 2 * d:3 * d]) * yb
    post = _rms(jnp.dot(merged.astype(BF16), wo_ref[...], preferred_element_type=F32), npost_ref[...])
    y_ref[...] = x_ref[...] + gate_ref[...] * post


def _sample(x, shift, scale, gate, sa, sb, s0, w):
    n, d = x.shape
    dqkv = 3 * d
    vm = pl.BlockSpec(memory_space=pltpu.VMEM)
    params = pltpu.CompilerParams(vmem_limit_bytes=V7X_VMEM_LIMIT_BYTES)
    row = jax.ShapeDtypeStruct((n, d), F32)
    front_in = (x, shift, scale, sa, sb, w["npre"], w["wmain"], w["wgate"], w["wsmall"], w["caw"], w["cbw"],
                w["alc"], w["dtc"], w["woa"])
    q, k, v, be, eg, ya, rest, nsa, nsb = pl.pallas_call(
        _sample_front_kernel,
        in_specs=[vm] * len(front_in),
        out_specs=[vm] * 9,
        out_shape=[row, row, row, row, row, row, jax.ShapeDtypeStruct((n, 3 * d), F32),
                   jax.ShapeDtypeStruct(sa.shape, F32), jax.ShapeDtypeStruct(sb.shape, F32)],
        compiler_params=params,
        name="sample_front",
    )(*front_in)

    bb = SAMPLE_BLOCK
    tok = pl.BlockSpec((bb, d), lambda i: (i, 0))
    st = pl.BlockSpec((bb, N_HEADS, HEAD, HEAD), lambda i: (i, 0, 0, 0))
    s_new, o = pl.pallas_call(
        _sample_state_kernel,
        grid=(n // bb,),
        in_specs=[st, tok, tok, tok, tok, tok],
        out_specs=[st, tok],
        out_shape=[jax.ShapeDtypeStruct(s0.shape, F32), row],
        compiler_params=pltpu.CompilerParams(dimension_semantics=("arbitrary",),
                                             vmem_limit_bytes=V7X_VMEM_LIMIT_BYTES),
        name="sample_state",
    )(s0, q, k, v, be, eg)

    back_in = (x, gate, o, ya, rest, w["onw"], w["wob"], w["wo"], w["npost"])
    y = pl.pallas_call(
        _sample_back_kernel,
        in_specs=[vm] * len(back_in),
        out_specs=vm,
        out_shape=row,
        compiler_params=params,
        name="sample_back",
    )(*back_in)
    return y, nsa, nsb, s_new


def kernel(x_prompt, x_sample, c_prompt, c_sample, state_conv_a, state_conv_qkv, state_delta, ada_w, ada_b,
           norm_pre, w_in, conv_a_w, conv_b_w, a_log, dt_bias, onorm_w, w_out_a, w_out_b, w_o, norm_post):
    depth = w_in.shape[0]
    assert depth == 1, "single-layer trunk"
    bp, seq, d = x_prompt.shape
    ns = x_sample.shape[0]
    dqkv = 3 * d
    assert d == N_HEADS * HEAD and seq % PROMPT_TILE == 0 and PROMPT_TILE % CHUNK == 0 and 2 * CHUNK == V7X_LANES
    assert ns % SAMPLE_BLOCK == 0 and x_sample.shape[1] == 1

    off_small = 4 * d + dqkv + d
    w0 = w_in.reshape(w_in.shape[1:])
    wmain = w0[:, :off_small].astype(BF16)
    wgate = w0[:, off_small + 2 * N_HEADS:].astype(BF16)
    wsmall_f = jnp.pad(w0[:, off_small:off_small + 2 * N_HEADS], ((0, 0), (0, V7X_LANES - 2 * N_HEADS)))
    wsmall = wsmall_f.astype(BF16)
    wsmallt = w0[:, off_small:off_small + 2 * N_HEADS].T.astype(BF16)
    zeros_h = jnp.zeros((N_HEADS,), F32)
    lane_pad = jnp.zeros((V7X_LANES - 2 * N_HEADS,), F32)
    w = {
        "npre": norm_pre, "npost": norm_post, "onw": onorm_w,
        "wmain": wmain, "wgate": wgate, "wsmall": wsmall, "wsmallt": wsmallt,
        "caw": conv_a_w[0], "cbw": conv_b_w[0],
        "alc": jnp.concatenate([zeros_h, a_log[0], lane_pad])[None, :],
        "dtc": jnp.concatenate([zeros_h, dt_bias[0], lane_pad])[None, :],
        "alr": jnp.concatenate([zeros_h, a_log[0]])[:, None],
        "dtr": jnp.concatenate([zeros_h, dt_bias[0]])[:, None],
        "woa": w_out_a[0].astype(BF16), "wob": w_out_b[0].astype(BF16), "wo": w_o[0].astype(BF16),
    }

    mod = _adaln(jnp.concatenate([c_prompt, c_sample], axis=0), ada_w.reshape(ada_w.shape[1:]), ada_b)
    shift, scale, gate = mod[:, 0:d], mod[:, d:2 * d], mod[:, 2 * d:3 * d]

    consts = (w["npre"], w["wmain"], w["wgate"], w["wsmall"], w["wsmallt"], w["caw"], w["cbw"], w["alc"], w["dtc"],
              w["alr"], w["dtr"], w["onw"], w["woa"], w["wob"], w["wo"], w["npost"])
    yp, pca, pcb, pds = _prompt(x_prompt, shift[:bp, None, :], scale[:bp, None, :], gate[:bp, None, :], consts)

    ys, nsa, nsb, sds = _sample(
        x_sample[:, 0, :], shift[bp:], scale[bp:], gate[bp:],
        state_conv_a.reshape(ns, (CONV_A_W - 1) * d), state_conv_qkv.reshape(ns, (CONV_B_W - 1) * dqkv),
        state_delta.reshape(state_delta.shape[1:]), w)

    return (yp, ys.reshape(ns, 1, d), pca.reshape((1,) + pca.shape), pcb.reshape((1,) + pcb.shape),
            pds.reshape((1,) + pds.shape).astype(state_delta.dtype),
            nsa.reshape(1, ns, CONV_A_W - 1, d), nsb.reshape(1, ns, CONV_B_W - 1, dqkv),
            sds.reshape((1,) + sds.shape).astype(state_delta.dtype))
```

```python
import functools

import jax
import jax.numpy as jnp
from jax import lax
from jax.experimental import pallas as pl
from jax.experimental.pallas import tpu as pltpu

F32 = jnp.float32
BF16 = jnp.bfloat16

EPS = 1e-6
N_HEADS = 8
HEAD = 128
CONV_A_W = 3
CONV_B_W = 4

V7X_LANES = 128
V7X_SUBLANES = 8
V7X_VMEM_LIMIT_BYTES = 60 * 1024 * 1024

PROMPT_TILE = 256
SIDE_COLS = 512
SIDE_EVERY_PREP = 4
SIDE_EVERY_SOLVE = 32
SIDE_EVERY = 16
CHUNK = 64
SAMPLE_BLOCK = 8

HI = lax.Precision.HIGHEST


NEG_LOG2_E = -1.4426950408889634


def _sigmoid(x):
    return 1.0 / (1.0 + jnp.exp2(x * NEG_LOG2_E))


def _silu(x):
    return x * _sigmoid(x)


def _softplus(x):
    return jnp.maximum(x, 0.0) + jnp.log1p(jnp.exp(-jnp.abs(x)))


def _dot(a, b):
    return jnp.dot(a.astype(BF16), b.astype(BF16), preferred_element_type=F32)


def _dot_nt(a, b):
    return lax.dot_general(a.astype(BF16), b.astype(BF16), (((1,), (1,)), ((), ())),
                           preferred_element_type=F32)


def _dot_tn(a, b):
    return lax.dot_general(a.astype(BF16), b.astype(BF16), (((0,), (0,)), ((), ())),
                           preferred_element_type=F32)


def _dot_exact(a, b):
    return jnp.dot(a, b, precision=HI, preferred_element_type=F32)


def _split(a):
    hi = a.astype(BF16)
    return hi, (a - hi.astype(F32)).astype(BF16)


def _solve_dot(a, b):
    ah, al = _split(a)
    bh, bl = _split(b)
    return (jnp.dot(ah, bh, preferred_element_type=F32) + jnp.dot(ah, bl, preferred_element_type=F32)
            + jnp.dot(al, bh, preferred_element_type=F32))


def _rms(x, w):
    return x * lax.rsqrt(jnp.mean(x * x, axis=-1, keepdims=True) + EPS) * w


def _head_l2norm(x, scale):
    outs = []
    for h in range(x.shape[1] // HEAD):
        xh = x[:, h * HEAD:(h + 1) * HEAD]
        outs.append(xh * (lax.rsqrt(jnp.sum(xh * xh, axis=-1, keepdims=True) + EPS) * scale))
    return jnp.concatenate(outs, axis=-1)


def _head_rms_gate(o, w, z):
    outs = []
    for h in range(N_HEADS):
        sl = slice(h * HEAD, (h + 1) * HEAD)
        oh = o[:, sl]
        outs.append(oh * lax.rsqrt(jnp.mean(oh * oh, axis=-1, keepdims=True) + EPS) * w * _silu(z[:, sl]))
    return jnp.concatenate(outs, axis=-1)


def _lane_bcast(x, col, width):
    return jnp.broadcast_to(x[:, col:col + 1], (x.shape[0], width))


def _adaln_kernel(c_ref, w_ref, b_ref, o_ref):
    o_ref[...] = _dot(_silu(c_ref[...]), w_ref[...]) + b_ref[...]


def _adaln(c_all, ada_w, ada_b):
    n, d = c_all.shape
    cols = ada_w.shape[1]
    return pl.pallas_call(
        _adaln_kernel,
        grid=(cols // d,),
        in_specs=[pl.BlockSpec((n, d), lambda j: (0, 0)),
                  pl.BlockSpec((d, d), lambda j: (0, j)),
                  pl.BlockSpec((1, d), lambda j: (0, j))],
        out_specs=pl.BlockSpec((n, d), lambda j: (0, j)),
        out_shape=jax.ShapeDtypeStruct((n, cols), F32),
        name="adaln",
    )(c_all, ada_w, ada_b)


def _prompt_kernel(x_ref, shift_ref, scale_ref, gate_ref, npre_ref, wmain_ref, wgate_ref, wsmall_ref, wsmallt_ref,
                   caw_ref, cbw_ref, alc_ref, dtc_ref, alr_ref, dtr_ref, onw_ref, woa_ref, wob_ref, wo_ref,
                   npost_ref,
                   y_ref, ca_ref, cb_ref, s_ref,
                   abuf, qkvbuf, s_scr, q_scr, k_scr, v_scr, o_scr, b_scr, gc_scr, gt_scr, gcr_scr,
                   r_scr, aqk_scr, rhs_scr, qd_scr, kd_scr, egt_scr, mp_scr, sb_scr, ob_scr, wu_scr,
                   pa_scr, pr_scr, yap_scr, ya_scr):
    lt = x_ref.shape[1]
    d = x_ref.shape[2]
    dqkv = qkvbuf.shape[1]
    n_chunks = lt // CHUNK
    pad = V7X_SUBLANES
    t = pl.program_id(1)

    @pl.when(t == 0)
    def _():
        abuf[0:pad, :] = jnp.zeros((pad, d), F32)
        qkvbuf[0:pad, :] = jnp.zeros((pad, dqkv), F32)
        s_scr[...] = jnp.zeros(s_scr.shape, F32)

    x = x_ref[0]
    h = _rms(x, npre_ref[...]) * (1.0 + scale_ref[0]) + shift_ref[0]
    hb = h.astype(BF16)

    side_work = []

    def side_proj(dst_ref, dst_off, w_ref, w_off, blk):
        def run():
            dst_ref[:, dst_off + blk * SIDE_COLS:dst_off + (blk + 1) * SIDE_COLS] = jnp.dot(
                hb, w_ref[:, w_off + blk * SIDE_COLS:w_off + (blk + 1) * SIDE_COLS], preferred_element_type=F32)
        return run

    def side_branch_a(blk):
        def run():
            cols = slice(blk * SIDE_COLS, (blk + 1) * SIDE_COLS)
            u = pa_scr[:, d + blk * SIDE_COLS:d + (blk + 1) * SIDE_COLS] * pa_scr[:, cols]
            abuf[pad:pad + lt, cols] = u
            ext = abuf[:, cols]
            conv = u * caw_ref[CONV_A_W - 1:CONV_A_W, cols]
            for j in range(CONV_A_W - 1):
                conv = conv + pltpu.roll(ext, CONV_A_W - 1 - j, axis=0)[pad:pad + lt] * caw_ref[j:j + 1, cols]
            b_a = pa_scr[:, 2 * d + blk * SIDE_COLS:2 * d + (blk + 1) * SIDE_COLS]
            z_a = pa_scr[:, 3 * d + blk * SIDE_COLS:3 * d + (blk + 1) * SIDE_COLS]
            yap_scr[:, cols] = ((b_a * conv) * _silu(z_a)).astype(BF16)
            ca_ref[0, :, cols] = abuf[pad + lt - (CONV_A_W - 1):pad + lt, cols]
            abuf[0:pad, cols] = abuf[lt:lt + pad, cols]
        return run

    def side_out_a(blk):
        def run():
            cols = slice(blk * SIDE_COLS, (blk + 1) * SIDE_COLS)
            ya_scr[:, cols] = jnp.dot(yap_scr[...], woa_ref[:, cols], preferred_element_type=F32)
        return run

    side_work += [side_proj(pa_scr, 0, wmain_ref, 0, blk) for blk in range(4 * d // SIDE_COLS)]
    side_work += [side_proj(pr_scr, 0, wmain_ref, 4 * d + dqkv, blk) for blk in range(d // SIDE_COLS)]
    side_work += [side_proj(pr_scr, d, wgate_ref, 0, blk) for blk in range(2 * d // SIDE_COLS)]
    side_work += [side_branch_a(blk) for blk in range(d // SIDE_COLS)]
    side_work += [side_out_a(blk) for blk in range(d // SIDE_COLS)]
    side_work.reverse()

    def tick():
        if side_work:
            side_work.pop()()

    def conv_qkv(blk):
        cols = slice(blk * SIDE_COLS, (blk + 1) * SIDE_COLS)
        ext = qkvbuf[:, cols]
        cq = ext[pad:pad + lt] * cbw_ref[CONV_B_W - 1:CONV_B_W, cols]
        for j in range(CONV_B_W - 1):
            cq = cq + pltpu.roll(ext, CONV_B_W - 1 - j, axis=0)[pad:pad + lt] * cbw_ref[j:j + 1, cols]
        cq = _silu(cq)
        cb_ref[0, :, cols] = qkvbuf[pad + lt - (CONV_B_W - 1):pad + lt, cols]
        qkvbuf[0:pad, cols] = qkvbuf[lt:lt + pad, cols]
        dst = slice((blk * SIDE_COLS) % d, (blk * SIDE_COLS) % d + SIDE_COLS)
        if blk * SIDE_COLS < d:
            q_scr[:, dst] = _head_l2norm(cq, HEAD ** -0.5)
        elif blk * SIDE_COLS < 2 * d:
            k_scr[:, dst] = _head_l2norm(cq, 1.0)
        else:
            v_scr[:, dst] = cq

    def proj_qkv(blk):
        cols = slice(blk * SIDE_COLS, (blk + 1) * SIDE_COLS)
        qkvbuf[pad:pad + lt, cols] = jnp.dot(hb, wmain_ref[:, 4 * d + blk * SIDE_COLS:4 * d + (blk + 1) * SIDE_COLS],
                                             preferred_element_type=F32)

    proj_qkv(0)
    for blk in range(dqkv // SIDE_COLS):
        if blk + 1 < dqkv // SIDE_COLS:
            proj_qkv(blk + 1)
        else:
            tick()
        conv_qkv(blk)

    ps = jnp.dot(hb, wsmall_ref[...], preferred_element_type=F32)
    pst = lax.dot_general(wsmallt_ref[...], hb, (((1,), (1,)), ((), ())), preferred_element_type=F32)
    b_scr[...] = _sigmoid(ps)
    g_col = -jnp.exp(alc_ref[...]) * _softplus(ps + dtc_ref[...])
    g_row = -jnp.exp(alr_ref[...]) * _softplus(pst + dtr_ref[...])
    row_in_chunk = lax.broadcasted_iota(jnp.int32, (lt, V7X_LANES), 0) % CHUNK
    gc_col = g_col
    shift = 1
    while shift < CHUNK:
        gc_col = gc_col + jnp.where(row_in_chunk >= shift, pltpu.roll(gc_col, shift, axis=0), 0.0)
        shift *= 2
    gc_scr[...] = gc_col
    gt_scr[...] = jnp.concatenate(
        [jnp.broadcast_to(gc_col[(c + 1) * CHUNK - 1:(c + 1) * CHUNK, :], (CHUNK, V7X_LANES))
         for c in range(n_chunks)], axis=0)
    rj = lax.broadcasted_iota(jnp.int32, (CHUNK, CHUNK), 0)
    cj = lax.broadcasted_iota(jnp.int32, (CHUNK, CHUNK), 1)
    triu = jnp.where(rj <= cj, 1.0, 0.0).astype(F32)
    for c in range(n_chunks):
        gcr_scr[c] = _dot_exact(g_row[:, c * CHUNK:(c + 1) * CHUNK], triu)
    tick()

    causal = rj >= cj
    strict = rj > cj
    eye = jnp.where(rj == cj, 1.0, 0.0).astype(F32)

    nb = n_chunks * N_HEADS
    lane = lax.broadcasted_iota(jnp.int32, (CHUNK, 2 * CHUNK), 1)
    low_half = lane < CHUNK
    eye_left = jnp.where(lane == lax.broadcasted_iota(jnp.int32, (CHUNK, 2 * CHUNK), 0), 1.0, 0.0).astype(BF16)

    for c in range(n_chunks):
        rows = slice(c * CHUNK, (c + 1) * CHUNK)
        beta_blk = b_scr[rows, :]
        gc_blk = gc_scr[rows, :]
        gt_blk = gt_scr[rows, :]
        gcr_blk = gcr_scr[c]
        for hd in range(N_HEADS):
            n = c * N_HEADS + hd
            sl = slice(hd * HEAD, (hd + 1) * HEAD)
            qh = q_scr[rows, sl]
            kh = k_scr[rows, sl]
            vh = v_scr[rows, sl]
            bcol = _lane_bcast(beta_blk, hd, HEAD)
            gcol = _lane_bcast(gc_blk, N_HEADS + hd, HEAD)
            gtot = _lane_bcast(gt_blk, N_HEADS + hd, HEAD)
            grow = gcr_blk[N_HEADS + hd:N_HEADS + hd + 1, :]
            decay = jnp.exp(jnp.where(causal, gcol[:, 0:CHUNK] - grow, -jnp.inf))
            egc = jnp.exp(gcol)
            kb = kh * bcol
            xm = -jnp.where(strict, _dot_nt(kb, kh) * decay, 0.0)
            r_scr[n] = jnp.concatenate([xm, eye], axis=-1)
            aqk_scr[n] = (_dot_nt(qh, kh) * decay).astype(BF16)
            rhs_scr[n] = jnp.concatenate([vh * bcol, kb * egc], axis=-1)
            qd_scr[n] = qh * egc
            kd_scr[n] = (kh * jnp.exp(gtot - gcol)).astype(BF16)
            egt_scr[n] = jnp.exp(gtot[0:V7X_SUBLANES, :])
            if n % SIDE_EVERY_PREP == SIDE_EVERY_PREP - 1:
                tick()

    def split_parts(r):
        hi = r.astype(BF16)
        hi32 = hi.astype(F32)
        lo32 = r - hi32
        return hi, hi32, lo32

    m = 1
    while m < CHUNK:
        for n in range(nb):
            r = r_scr[n]
            hi, hi32, lo32 = split_parts(r)
            lhs = jnp.where(low_half, hi32, pltpu.roll(lo32, CHUNK, axis=1)).astype(BF16)
            lo = lo32.astype(BF16)
            him = hi - eye_left
            prod = jnp.dot(jnp.concatenate([lhs, lhs], axis=1), jnp.concatenate([him, him, lo, lo], axis=0),
                           preferred_element_type=F32)
            r_scr[n] = prod + r
            if n % SIDE_EVERY_SOLVE == SIDE_EVERY_SOLVE - 1:
                tick()
        m *= 2
    for n in range(nb):
        hi, hi32, lo32 = split_parts(r_scr[n])
        lhs = jnp.where(low_half, pltpu.roll(lo32, CHUNK, axis=1), hi32).astype(BF16)
        bh, _, bl32 = split_parts(rhs_scr[n])
        bl = bl32.astype(BF16)
        sol = jnp.dot(jnp.concatenate([lhs, lhs], axis=1), jnp.concatenate([bh, bh, bl, bl], axis=0),
                      preferred_element_type=F32)
        wu_scr[n] = jnp.concatenate([sol[:, HEAD:2 * HEAD], sol[:, 0:HEAD]], axis=1).astype(BF16)
        if n % SIDE_EVERY == SIDE_EVERY - 1:
            tick()
    for n in range(nb):
        wu = wu_scr[n]
        kw = lax.dot_general(kd_scr[n], wu, (((0,), (0,)), ((), ())), preferred_element_type=F32)
        aw = jnp.dot(aqk_scr[n], wu, preferred_element_type=F32)
        mp_scr[n, 0:HEAD, :] = kw[:, 0:HEAD].astype(BF16)
        mp_scr[n, HEAD:HEAD + CHUNK, :] = (qd_scr[n] - aw[:, 0:HEAD]).astype(BF16)
        sb_scr[n] = kw[:, HEAD:2 * HEAD]
        ob_scr[n] = aw[:, HEAD:2 * HEAD]
        if n % SIDE_EVERY == SIDE_EVERY - 1:
            tick()

    for c in range(n_chunks):
        rows = slice(c * CHUNK, (c + 1) * CHUNK)
        for hd in range(N_HEADS):
            n = c * N_HEADS + hd
            sl = slice(hd * HEAD, (hd + 1) * HEAD)
            s_old = s_scr[hd]
            res = jnp.dot(mp_scr[n], s_old.astype(BF16), preferred_element_type=F32)
            o_scr[rows, sl] = res[HEAD:HEAD + CHUNK] + ob_scr[n]
            s_scr[hd] = (s_old * jnp.concatenate([egt_scr[n]] * (HEAD // V7X_SUBLANES), axis=0)
                         - res[0:HEAD] + sb_scr[n])
        tick()
    while side_work:
        tick()

    on = _head_rms_gate(o_scr[...], onw_ref[...], pr_scr[:, 0:d])
    yb = jnp.dot(on.astype(BF16), wob_ref[...], preferred_element_type=F32)
    merged = _sigmoid(pr_scr[:, d:2 * d]) * ya_scr[...] + _sigmoid(pr_scr[:, 2 * d:3 * d]) * yb
    post = _rms(jnp.dot(merged.astype(BF16), wo_ref[...], preferred_element_type=F32), npost_ref[...])
    y_ref[0] = x + gate_ref[0] * post

    @pl.when(t == pl.num_programs(1) - 1)
    def _():
        s_ref[0] = s_scr[...]


def _prompt(x, shift, scale, gate, consts):
    bsz, seq, d = x.shape
    dqkv = 3 * d
    lt = PROMPT_TILE
    pad = V7X_SUBLANES
    nb = (lt // CHUNK) * N_HEADS
    vm = pl.BlockSpec(memory_space=pltpu.VMEM)
    per_b = pl.BlockSpec((1, 1, d), lambda b, t: (b, 0, 0))
    return pl.pallas_call(
        _prompt_kernel,
        grid=(bsz, seq // lt),
        in_specs=[pl.BlockSpec((1, lt, d), lambda b, t: (b, t, 0)), per_b, per_b, per_b] + [vm] * len(consts),
        out_specs=[pl.BlockSpec((1, lt, d), lambda b, t: (b, t, 0)),
                   pl.BlockSpec((1, CONV_A_W - 1, d), lambda b, t: (b, 0, 0)),
                   pl.BlockSpec((1, CONV_B_W - 1, dqkv), lambda b, t: (b, 0, 0)),
                   pl.BlockSpec((1, N_HEADS, HEAD, HEAD), lambda b, t: (b, 0, 0, 0))],
        out_shape=[jax.ShapeDtypeStruct((bsz, seq, d), F32),
                   jax.ShapeDtypeStruct((bsz, CONV_A_W - 1, d), F32),
                   jax.ShapeDtypeStruct((bsz, CONV_B_W - 1, dqkv), F32),
                   jax.ShapeDtypeStruct((bsz, N_HEADS, HEAD, HEAD), F32)],
        scratch_shapes=[pltpu.VMEM((pad + lt, d), F32),
                        pltpu.VMEM((pad + lt, dqkv), F32),
                        pltpu.VMEM((N_HEADS, HEAD, HEAD), F32),
                        pltpu.VMEM((lt, d), F32), pltpu.VMEM((lt, d), F32),
                        pltpu.VMEM((lt, d), F32), pltpu.VMEM((lt, d), F32),
                        pltpu.VMEM((lt, V7X_LANES), F32), pltpu.VMEM((lt, V7X_LANES), F32),
                        pltpu.VMEM((lt, V7X_LANES), F32),
                        pltpu.VMEM((lt // CHUNK, 2 * N_HEADS, CHUNK), F32),
                        pltpu.VMEM((nb, CHUNK, 2 * CHUNK), F32),
                        pltpu.VMEM((nb, CHUNK, CHUNK), BF16), pltpu.VMEM((nb, CHUNK, 2 * HEAD), F32),
                        pltpu.VMEM((nb, CHUNK, HEAD), F32), pltpu.VMEM((nb, CHUNK, HEAD), BF16),
                        pltpu.VMEM((nb, V7X_SUBLANES, HEAD), F32), pltpu.VMEM((nb, HEAD + CHUNK, HEAD), BF16),
                        pltpu.VMEM((nb, HEAD, HEAD), F32), pltpu.VMEM((nb, CHUNK, HEAD), F32),
                        pltpu.VMEM((nb, CHUNK, 2 * HEAD), BF16),
                        pltpu.VMEM((lt, 4 * d), F32), pltpu.VMEM((lt, 3 * d), F32),
                        pltpu.VMEM((lt, d), BF16), pltpu.VMEM((lt, d), F32)],
        compiler_params=pltpu.CompilerParams(dimension_semantics=("arbitrary", "arbitrary"),
                                             vmem_limit_bytes=V7X_VMEM_LIMIT_BYTES),
        name="prompt",
    )(x, shift, scale, gate, *consts)


def _sample_front_kernel(x_ref, shift_ref, scale_ref, sa_ref, sb_ref, npre_ref, wmain_ref, wgate_ref, wsmall_ref,
                         caw_ref, cbw_ref, alc_ref, dtc_ref, woa_ref,
                         q_ref, k_ref, v_ref, be_ref, eg_ref, ya_ref, rest_ref, nsa_ref, nsb_ref):
    d = x_ref.shape[1]
    dqkv = 3 * d
    n = x_ref.shape[0]
    x = x_ref[...]
    h = _rms(x, npre_ref[...]) * (1.0 + scale_ref[...]) + shift_ref[...]
    hb = h.astype(BF16)

    pa = jnp.dot(hb, wmain_ref[:, 0:4 * d], preferred_element_type=F32)
    u = pa[:, d:2 * d] * pa[:, 0:d]
    conv = u * caw_ref[CONV_A_W - 1:CONV_A_W, :]
    for j in range(CONV_A_W - 1):
        conv = conv + sa_ref[j] * caw_ref[j:j + 1, :]
    ya_ref[...] = jnp.dot(((pa[:, 2 * d:3 * d] * conv) * _silu(pa[:, 3 * d:4 * d])).astype(BF16), woa_ref[...],
                          preferred_element_type=F32)
    for j in range(CONV_A_W - 2):
        nsa_ref[j] = sa_ref[j + 1]
    nsa_ref[CONV_A_W - 2] = u

    pq = jnp.dot(hb, wmain_ref[:, 4 * d:4 * d + dqkv], preferred_element_type=F32)
    cq = pq * cbw_ref[CONV_B_W - 1:CONV_B_W, :]
    for j in range(CONV_B_W - 1):
        cq = cq + sb_ref[j] * cbw_ref[j:j + 1, :]
    cq = _silu(cq)
    for j in range(CONV_B_W - 2):
        nsb_ref[j] = sb_ref[j + 1]
    nsb_ref[CONV_B_W - 2] = pq
    q_ref[...] = _head_l2norm(cq[:, 0:d], HEAD ** -0.5)
    k_ref[...] = _head_l2norm(cq[:, d:2 * d], 1.0)
    v_ref[...] = cq[:, 2 * d:3 * d]

    rest_ref[:, 0:d] = jnp.dot(hb, wmain_ref[:, 4 * d + dqkv:5 * d + dqkv], preferred_element_type=F32)
    rest_ref[:, d:3 * d] = jnp.dot(hb, wgate_ref[...], preferred_element_type=F32)

    ps = jnp.dot(hb, wsmall_ref[...], preferred_element_type=F32)
    beta = _sigmoid(ps)
    eg = jnp.exp(-jnp.exp(alc_ref[...]) * _softplus(ps + dtc_ref[...]))
    be_ref[...] = jnp.concatenate([_lane_bcast(beta, hd, HEAD) for hd in range(N_HEADS)], axis=-1)
    eg_ref[...] = jnp.concatenate([_lane_bcast(eg, N_HEADS + hd, HEAD) for hd in range(N_HEADS)], axis=-1)


def _sample_state_kernel(s_ref, q_ref, k_ref, v_ref, be_ref, eg_ref, so_ref, o_ref):
    bb = q_ref.shape[0]
    row = lax.broadcasted_iota(jnp.int32, (bb, HEAD), 0)
    erow = lax.broadcasted_iota(jnp.int32, (bb, bb * HEAD), 0)
    elane = lax.broadcasted_iota(jnp.int32, (bb, bb * HEAD), 1)
    diag = (elane // HEAD) == erow
    zeros_pad = jnp.zeros((2 * V7X_SUBLANES - bb, HEAD), F32)
    for hd in range(N_HEADS):
        sl = slice(hd * HEAD, (hd + 1) * HEAD)
        qh = q_ref[:, sl]
        kh = k_ref[:, sl]
        vh = v_ref[:, sl]
        eg = eg_ref[:, sl]
        kq = jnp.concatenate([kh, qh], axis=0).astype(BF16)
        ks = jnp.zeros((bb, HEAD), F32)
        qs = jnp.zeros((bb, HEAD), F32)
        for tk in range(bb):
            r = jnp.dot(kq, s_ref[tk, hd].astype(BF16), preferred_element_type=F32)
            ks = jnp.where(row == tk, r[0:bb], ks)
            qs = jnp.where(row == tk, r[bb:2 * bb], qs)
        vn = be_ref[:, sl] * (vh - eg * ks)
        o_ref[:, sl] = eg * qs + jnp.sum(qh * kh, axis=-1, keepdims=True) * vn
        vexp = jnp.where(diag, jnp.concatenate([vn] * bb, axis=-1), 0.0)
        outer = _dot_tn(jnp.concatenate([kh, zeros_pad], axis=0),
                        jnp.concatenate([vexp, jnp.zeros((2 * V7X_SUBLANES - bb, bb * HEAD), F32)], axis=0))
        for tk in range(bb):
            so_ref[tk, hd] = (s_ref[tk, hd] * jnp.broadcast_to(eg[tk:tk + 1, :], (HEAD, HEAD))
                              + outer[:, tk * HEAD:(tk + 1) * HEAD])


def _sample_back_kernel(x_ref, gate_ref, o_ref, ya_ref, rest_ref, onw_ref, wob_ref, wo_ref, npost_ref, y_ref):
    d = x_ref.shape[1]
    on = _head_rms_gate(o_ref[...], onw_ref[...], rest_ref[:, 0:d])
    yb = jnp.dot(on.astype(BF16), wob_ref[...], preferred_element_type=F32)
    merged = _sigmoid(rest_ref[:, d:2 * d]) * ya_ref[...] + _sigmoid(rest_ref[:, 2 * d:3 * d]) * yb
    post = _rms(jnp.dot(merged.astype(BF16), wo_ref[...], preferred_element_type=F32), npost_ref[...])
    y_ref[...] = x_ref[...] + gate_ref[...] * post


def _sample(x, shift, scale, gate, sa, sb, s0, w):
    n, d = x.shape
    dqkv = 3 * d
    vm = pl.BlockSpec(memory_space=pltpu.VMEM)
    params = pltpu.CompilerParams(vmem_limit_bytes=V7X_VMEM_LIMIT_BYTES)
    row = jax.ShapeDtypeStruct((n, d), F32)
    front_in = (x, shift, scale, sa, sb, w["npre"], w["wmain"], w["wgate"], w["wsmall"], w["caw"], w["cbw"],
                w["alc"], w["dtc"], w["woa"])
    q, k, v, be, eg, ya, rest, nsa, nsb = pl.pallas_call(
        _sample_front_kernel,
        in_specs=[vm] * len(front_in),
        out_specs=[vm] * 9,
        out_shape=[row, row, row, row, row, row, jax.ShapeDtypeStruct((n, 3 * d), F32),
                   jax.ShapeDtypeStruct(sa.shape, F32), jax.ShapeDtypeStruct(sb.shape, F32)],
        compiler_params=params,
        name="sample_front",
    )(*front_in)

    bb = SAMPLE_BLOCK
    tok = pl.BlockSpec((bb, d), lambda i: (i, 0))
    st = pl.BlockSpec((bb, N_HEADS, HEAD, HEAD), lambda i: (i, 0, 0, 0))
    s_new, o = pl.pallas_call(
        _sample_state_kernel,
        grid=(n // bb,),
        in_specs=[st, tok, tok, tok, tok, tok],
        out_specs=[st, tok],
        out_shape=[jax.ShapeDtypeStruct(s0.shape, F32), row],
        compiler_params=pltpu.CompilerParams(dimension_semantics=("arbitrary",),
                                             vmem_limit_bytes=V7X_VMEM_LIMIT_BYTES),
        name="sample_state",
    )(s0, q, k, v, be, eg)

    back_in = (x, gate, o, ya, rest, w["onw"], w["wob"], w["wo"], w["npost"])
    y = pl.pallas_call(
        _sample_back_kernel,
        in_specs=[vm] * len(back_in),
        out_specs=vm,
        out_shape=row,
        compiler_params=params,
        name="sample_back",
    )(*back_in)
    return y, nsa, nsb, s_new


def kernel(x_prompt, x_sample, c_prompt, c_sample, state_conv_a, state_conv_qkv, state_delta, ada_w, ada_b,
           norm_pre, w_in, conv_a_w, conv_b_w, a_log, dt_bias, onorm_w, w_out_a, w_out_b, w_o, norm_post):
    depth = w_in.shape[0]
    assert depth == 1, "single-layer trunk"
    bp, seq, d = x_prompt.shape
    ns = x_sample.shape[0]
    dqkv = 3 * d
    assert d == N_HEADS * HEAD and seq % PROMPT_TILE == 0 and PROMPT_TILE % CHUNK == 0 and 2 * CHUNK == V7X_LANES
    assert ns % SAMPLE_BLOCK == 0 and x_sample.shape[1] == 1

    off_small = 4 * d + dqkv + d
    w0 = w_in.reshape(w_in.shape[1:])
    wmain = w0.astype(BF16)
    wgate = wmain[:, off_small + 2 * N_HEADS:]
    wsmall = jnp.pad(wmain[:, off_small:off_small + 2 * N_HEADS], ((0, 0), (0, V7X_LANES - 2 * N_HEADS)))
    wsmallt = wmain[:, off_small:off_small + 2 * N_HEADS].T
    zeros_h = jnp.zeros((N_HEADS,), F32)
    lane_pad = jnp.zeros((V7X_LANES - 2 * N_HEADS,), F32)
    w = {
        "npre": norm_pre, "npost": norm_post, "onw": onorm_w,
        "wmain": wmain, "wgate": wgate, "wsmall": wsmall, "wsmallt": wsmallt,
        "caw": conv_a_w[0], "cbw": conv_b_w[0],
        "alc": jnp.concatenate([zeros_h, a_log[0], lane_pad])[None, :],
        "dtc": jnp.concatenate([zeros_h, dt_bias[0], lane_pad])[None, :],
        "alr": jnp.concatenate([zeros_h, a_log[0]])[:, None],
        "dtr": jnp.concatenate([zeros_h, dt_bias[0]])[:, None],
        "woa": w_out_a[0].astype(BF16), "wob": w_out_b[0].astype(BF16), "wo": w_o[0].astype(BF16),
    }

    mod = _adaln(jnp.concatenate([c_prompt, c_sample], axis=0), ada_w.reshape(ada_w.shape[1:]), ada_b)
    shift, scale, gate = mod[:, 0:d], mod[:, d:2 * d], mod[:, 2 * d:3 * d]

    consts = (w["npre"], w["wmain"], w["wgate"], w["wsmall"], w["wsmallt"], w["caw"], w["cbw"], w["alc"], w["dtc"],
              w["alr"], w["dtr"], w["onw"], w["woa"], w["wob"], w["wo"], w["npost"])
    yp, pca, pcb, pds = _prompt(x_prompt, shift[:bp, None, :], scale[:bp, None, :], gate[:bp, None, :], consts)

    ys, nsa, nsb, sds = _sample(
        x_sample[:, 0, :], shift[bp:], scale[bp:], gate[bp:],
        jnp.transpose(state_conv_a.reshape(ns, CONV_A_W - 1, d), (1, 0, 2)),
        jnp.transpose(state_conv_qkv.reshape(ns, CONV_B_W - 1, dqkv), (1, 0, 2)),
        state_delta.reshape(state_delta.shape[1:]), w)

    return (yp, ys.reshape(ns, 1, d), pca.reshape((1,) + pca.shape), pcb.reshape((1,) + pcb.shape),
            pds.reshape((1,) + pds.shape).astype(state_delta.dtype),
            jnp.transpose(nsa, (1, 0, 2)).reshape(1, ns, CONV_A_W - 1, d),
            jnp.transpose(nsb, (1, 0, 2)).reshape(1, ns, CONV_B_W - 1, dqkv),
            sds.reshape((1,) + sds.shape).astype(state_delta.dtype))
```

```python
import functools

import jax
import jax.numpy as jnp
from jax import lax
from jax.experimental import pallas as pl
from jax.experimental.pallas import tpu as pltpu

F32 = jnp.float32
BF16 = jnp.bfloat16

EPS = 1e-6
N_HEADS = 8
HEAD = 128
CONV_A_W = 3
CONV_B_W = 4

V7X_LANES = 128
V7X_SUBLANES = 8
V7X_VMEM_LIMIT_BYTES = 60 * 1024 * 1024

PROMPT_TILE = 256
SIDE_COLS = 512
SIDE_EVERY_PREP = 4
SIDE_EVERY_SOLVE = 32
SIDE_EVERY = 16
CHUNK = 64
SAMPLE_BLOCK = 8

HI = lax.Precision.HIGHEST


NEG_LOG2_E = -1.4426950408889634


def _sigmoid(x):
    return 1.0 / (1.0 + jnp.exp2(x * NEG_LOG2_E))


def _silu(x):
    return x * _sigmoid(x)


def _softplus(x):
    return jnp.maximum(x, 0.0) + jnp.log1p(jnp.exp(-jnp.abs(x)))


def _dot(a, b):
    return jnp.dot(a.astype(BF16), b.astype(BF16), preferred_element_type=F32)


def _dot_nt(a, b):
    return lax.dot_general(a.astype(BF16), b.astype(BF16), (((1,), (1,)), ((), ())),
                           preferred_element_type=F32)


def _dot_tn(a, b):
    return lax.dot_general(a.astype(BF16), b.astype(BF16), (((0,), (0,)), ((), ())),
                           preferred_element_type=F32)


def _dot_exact(a, b):
    return jnp.dot(a, b, precision=HI, preferred_element_type=F32)


def _split(a):
    hi = a.astype(BF16)
    return hi, (a - hi.astype(F32)).astype(BF16)


def _solve_dot(a, b):
    ah, al = _split(a)
    bh, bl = _split(b)
    return (jnp.dot(ah, bh, preferred_element_type=F32) + jnp.dot(ah, bl, preferred_element_type=F32)
            + jnp.dot(al, bh, preferred_element_type=F32))


def _rms(x, w):
    return x * lax.rsqrt(jnp.mean(x * x, axis=-1, keepdims=True) + EPS) * w


def _head_l2norm(x, scale):
    outs = []
    for h in range(x.shape[1] // HEAD):
        xh = x[:, h * HEAD:(h + 1) * HEAD]
        outs.append(xh * (lax.rsqrt(jnp.sum(xh * xh, axis=-1, keepdims=True) + EPS) * scale))
    return jnp.concatenate(outs, axis=-1)


def _head_rms_gate(o, w, z):
    outs = []
    for h in range(N_HEADS):
        sl = slice(h * HEAD, (h + 1) * HEAD)
        oh = o[:, sl]
        outs.append(oh * lax.rsqrt(jnp.mean(oh * oh, axis=-1, keepdims=True) + EPS) * w * _silu(z[:, sl]))
    return jnp.concatenate(outs, axis=-1)


def _causal_taps(buf_ref, cols, w_ref, width, rows, zero):
    sub = V7X_SUBLANES
    n_tiles = rows // sub
    tiles = [buf_ref[i * sub:(i + 1) * sub, cols] for i in range(n_tiles + 1)]
    width_cols = tiles[0].shape[1]
    row = lax.broadcasted_iota(jnp.int32, (sub, width_cols), 0)
    w_last = jnp.broadcast_to(w_ref[width - 1:width, cols], (sub, width_cols))
    acc = [tiles[i + 1] * w_last for i in range(n_tiles)]
    for j in range(width - 1):
        s = width - 1 - j
        w_j = jnp.broadcast_to(w_ref[j:j + 1, cols], (sub, width_cols))
        rots = [pltpu.roll(tl, s + zero, axis=0) for tl in tiles]
        from_above = row < s
        acc = [acc[i] + jnp.where(from_above, rots[i], rots[i + 1]) * w_j for i in range(n_tiles)]
    return jnp.concatenate(acc, axis=0)


def _lane_bcast(x, col, width):
    return jnp.broadcast_to(x[:, col:col + 1], (x.shape[0], width))


def _adaln_kernel(c_ref, w_ref, b_ref, o_ref):
    o_ref[...] = _dot(_silu(c_ref[...]), w_ref[...]) + b_ref[...]


def _adaln(c_all, ada_w, ada_b):
    n, d = c_all.shape
    cols = ada_w.shape[1]
    return pl.pallas_call(
        _adaln_kernel,
        grid=(cols // d,),
        in_specs=[pl.BlockSpec((n, d), lambda j: (0, 0)),
                  pl.BlockSpec((d, d), lambda j: (0, j)),
                  pl.BlockSpec((1, d), lambda j: (0, j))],
        out_specs=pl.BlockSpec((n, d), lambda j: (0, j)),
        out_shape=jax.ShapeDtypeStruct((n, cols), F32),
        name="adaln",
    )(c_all, ada_w, ada_b)


def _prompt_kernel(x_ref, shift_ref, scale_ref, gate_ref, npre_ref, wmain_ref, wgate_ref, wsmall_ref, wsmallt_ref,
                   caw_ref, cbw_ref, alc_ref, dtc_ref, alr_ref, dtr_ref, onw_ref, woa_ref, wob_ref, wo_ref,
                   npost_ref,
                   y_ref, ca_ref, cb_ref, s_ref,
                   abuf, qkvbuf, s_scr, q_scr, k_scr, v_scr, o_scr, b_scr, gc_scr, gt_scr, gcr_scr,
                   r_scr, aqk_scr, rhs_scr, qd_scr, kd_scr, egt_scr, mp_scr, sb_scr, ob_scr, wu_scr,
                   pa_scr, pr_scr, yap_scr, ya_scr):
    lt = x_ref.shape[1]
    d = x_ref.shape[2]
    dqkv = qkvbuf.shape[1]
    n_chunks = lt // CHUNK
    pad = V7X_SUBLANES
    t = pl.program_id(1)

    @pl.when(t == 0)
    def _():
        abuf[0:pad, :] = jnp.zeros((pad, d), F32)
        qkvbuf[0:pad, :] = jnp.zeros((pad, dqkv), F32)
        s_scr[...] = jnp.zeros(s_scr.shape, F32)

    x = x_ref[0]
    h = _rms(x, npre_ref[...]) * (1.0 + scale_ref[0]) + shift_ref[0]
    hb = h.astype(BF16)

    side_work = []

    def side_proj(dst_ref, dst_off, w_ref, w_off, blk):
        def run():
            dst_ref[:, dst_off + blk * SIDE_COLS:dst_off + (blk + 1) * SIDE_COLS] = jnp.dot(
                hb, w_ref[:, w_off + blk * SIDE_COLS:w_off + (blk + 1) * SIDE_COLS], preferred_element_type=F32)
        return run

    def side_branch_a(blk):
        def run():
            cols = slice(blk * SIDE_COLS, (blk + 1) * SIDE_COLS)
            u = pa_scr[:, d + blk * SIDE_COLS:d + (blk + 1) * SIDE_COLS] * pa_scr[:, cols]
            abuf[pad:pad + lt, cols] = u
            conv = _causal_taps(abuf, cols, caw_ref, CONV_A_W, lt, jnp.minimum(t, 0))
            b_a = pa_scr[:, 2 * d + blk * SIDE_COLS:2 * d + (blk + 1) * SIDE_COLS]
            z_a = pa_scr[:, 3 * d + blk * SIDE_COLS:3 * d + (blk + 1) * SIDE_COLS]
            yap_scr[:, cols] = ((b_a * conv) * _silu(z_a)).astype(BF16)
            ca_ref[0, :, cols] = abuf[pad + lt - (CONV_A_W - 1):pad + lt, cols]
            abuf[0:pad, cols] = abuf[lt:lt + pad, cols]
        return run

    def side_out_a(blk):
        def run():
            cols = slice(blk * SIDE_COLS, (blk + 1) * SIDE_COLS)
            ya_scr[:, cols] = jnp.dot(yap_scr[...], woa_ref[:, cols], preferred_element_type=F32)
        return run

    side_work += [side_proj(pa_scr, 0, wmain_ref, 0, blk) for blk in range(4 * d // SIDE_COLS)]
    side_work += [side_proj(pr_scr, 0, wmain_ref, 4 * d + dqkv, blk) for blk in range(d // SIDE_COLS)]
    side_work += [side_proj(pr_scr, d, wgate_ref, 0, blk) for blk in range(2 * d // SIDE_COLS)]
    side_work += [side_branch_a(blk) for blk in range(d // SIDE_COLS)]
    side_work += [side_out_a(blk) for blk in range(d // SIDE_COLS)]
    side_work.reverse()

    def tick():
        if side_work:
            side_work.pop()()

    def conv_qkv(blk):
        cols = slice(blk * SIDE_COLS, (blk + 1) * SIDE_COLS)
        cq = _silu(_causal_taps(qkvbuf, cols, cbw_ref, CONV_B_W, lt, jnp.minimum(t, 0)))
        cb_ref[0, :, cols] = qkvbuf[pad + lt - (CONV_B_W - 1):pad + lt, cols]
        qkvbuf[0:pad, cols] = qkvbuf[lt:lt + pad, cols]
        dst = slice((blk * SIDE_COLS) % d, (blk * SIDE_COLS) % d + SIDE_COLS)
        if blk * SIDE_COLS < d:
            q_scr[:, dst] = _head_l2norm(cq, HEAD ** -0.5)
        elif blk * SIDE_COLS < 2 * d:
            k_scr[:, dst] = _head_l2norm(cq, 1.0)
        else:
            v_scr[:, dst] = cq

    def proj_qkv(blk):
        cols = slice(blk * SIDE_COLS, (blk + 1) * SIDE_COLS)
        qkvbuf[pad:pad + lt, cols] = jnp.dot(hb, wmain_ref[:, 4 * d + blk * SIDE_COLS:4 * d + (blk + 1) * SIDE_COLS],
                                             preferred_element_type=F32)

    proj_qkv(0)
    for blk in range(dqkv // SIDE_COLS):
        if blk + 1 < dqkv // SIDE_COLS:
            proj_qkv(blk + 1)
        else:
            tick()
        conv_qkv(blk)

    ps = jnp.dot(hb, wsmall_ref[...], preferred_element_type=F32)
    pst = lax.dot_general(wsmallt_ref[...], hb, (((1,), (1,)), ((), ())), preferred_element_type=F32)
    b_scr[...] = _sigmoid(ps)
    g_col = -jnp.exp(alc_ref[...]) * _softplus(ps + dtc_ref[...])
    g_row = -jnp.exp(alr_ref[...]) * _softplus(pst + dtr_ref[...])
    row_in_chunk = lax.broadcasted_iota(jnp.int32, (lt, V7X_LANES), 0) % CHUNK
    gc_col = g_col
    shift = 1
    while shift < CHUNK:
        gc_col = gc_col + jnp.where(row_in_chunk >= shift, pltpu.roll(gc_col, shift, axis=0), 0.0)
        shift *= 2
    gc_scr[...] = gc_col
    gt_scr[...] = jnp.concatenate(
        [jnp.broadcast_to(gc_col[(c + 1) * CHUNK - 1:(c + 1) * CHUNK, :], (CHUNK, V7X_LANES))
         for c in range(n_chunks)], axis=0)
    rj = lax.broadcasted_iota(jnp.int32, (CHUNK, CHUNK), 0)
    cj = lax.broadcasted_iota(jnp.int32, (CHUNK, CHUNK), 1)
    triu = jnp.where(rj <= cj, 1.0, 0.0).astype(F32)
    for c in range(n_chunks):
        gcr_scr[c] = _dot_exact(g_row[:, c * CHUNK:(c + 1) * CHUNK], triu)
    tick()

    causal = rj >= cj
    strict = rj > cj
    eye = jnp.where(rj == cj, 1.0, 0.0).astype(F32)

    nb = n_chunks * N_HEADS
    lane = lax.broadcasted_iota(jnp.int32, (CHUNK, 2 * CHUNK), 1)
    low_half = lane < CHUNK
    eye_left = jnp.where(lane == lax.broadcasted_iota(jnp.int32, (CHUNK, 2 * CHUNK), 0), 1.0, 0.0).astype(BF16)

    for c in range(n_chunks):
        rows = slice(c * CHUNK, (c + 1) * CHUNK)
        beta_blk = b_scr[rows, :]
        gc_blk = gc_scr[rows, :]
        gt_blk = gt_scr[rows, :]
        gcr_blk = gcr_scr[c]
        for hd in range(N_HEADS):
            n = c * N_HEADS + hd
            sl = slice(hd * HEAD, (hd + 1) * HEAD)
            qh = q_scr[rows, sl]
            kh = k_scr[rows, sl]
            vh = v_scr[rows, sl]
            bcol = _lane_bcast(beta_blk, hd, HEAD)
            gcol = _lane_bcast(gc_blk, N_HEADS + hd, HEAD)
            gtot = _lane_bcast(gt_blk, N_HEADS + hd, HEAD)
            grow = gcr_blk[N_HEADS + hd:N_HEADS + hd + 1, :]
            decay = jnp.exp(jnp.where(causal, gcol[:, 0:CHUNK] - grow, -jnp.inf))
            egc = jnp.exp(gcol)
            kb = kh * bcol
            xm = -jnp.where(strict, _dot_nt(kb, kh) * decay, 0.0)
            r_scr[n] = jnp.concatenate([xm, eye], axis=-1)
            aqk_scr[n] = (_dot_nt(qh, kh) * decay).astype(BF16)
            rhs_scr[n] = jnp.concatenate([vh * bcol, kb * egc], axis=-1)
            qd_scr[n] = qh * egc
            kd_scr[n] = (kh * jnp.exp(gtot - gcol)).astype(BF16)
            egt_scr[n] = jnp.exp(gtot[0:V7X_SUBLANES, :])
            if n % SIDE_EVERY_PREP == SIDE_EVERY_PREP - 1:
                tick()

    def split_parts(r):
        hi = r.astype(BF16)
        hi32 = hi.astype(F32)
        lo32 = r - hi32
        return hi, hi32, lo32

    m = 1
    while m < CHUNK:
        for n in range(nb):
            r = r_scr[n]
            hi, hi32, lo32 = split_parts(r)
            lhs = jnp.where(low_half, hi32, pltpu.roll(lo32, CHUNK, axis=1)).astype(BF16)
            lo = lo32.astype(BF16)
            him = hi - eye_left
            prod = jnp.dot(jnp.concatenate([lhs, lhs], axis=1), jnp.concatenate([him, him, lo, lo], axis=0),
                           preferred_element_type=F32)
            r_scr[n] = prod + r
            if n % SIDE_EVERY_SOLVE == SIDE_EVERY_SOLVE - 1:
                tick()
        m *= 2
    for n in range(nb):
        hi, hi32, lo32 = split_parts(r_scr[n])
        lhs = jnp.where(low_half, pltpu.roll(lo32, CHUNK, axis=1), hi32).astype(BF16)
        bh, _, bl32 = split_parts(rhs_scr[n])
        bl = bl32.astype(BF16)
        sol = jnp.dot(jnp.concatenate([lhs, lhs], axis=1), jnp.concatenate([bh, bh, bl, bl], axis=0),
                      preferred_element_type=F32)
        wu_scr[n] = jnp.concatenate([sol[:, HEAD:2 * HEAD], sol[:, 0:HEAD]], axis=1).astype(BF16)
        if n % SIDE_EVERY == SIDE_EVERY - 1:
            tick()
    for n in range(nb):
        wu = wu_scr[n]
        kw = lax.dot_general(kd_scr[n], wu, (((0,), (0,)), ((), ())), preferred_element_type=F32)
        aw = jnp.dot(aqk_scr[n], wu, preferred_element_type=F32)
        mp_scr[n, 0:HEAD, :] = kw[:, 0:HEAD].astype(BF16)
        mp_scr[n, HEAD:HEAD + CHUNK, :] = (qd_scr[n] - aw[:, 0:HEAD]).astype(BF16)
        sb_scr[n] = kw[:, HEAD:2 * HEAD]
        ob_scr[n] = aw[:, HEAD:2 * HEAD]
        if n % SIDE_EVERY == SIDE_EVERY - 1:
            tick()

    for c in range(n_chunks):
        rows = slice(c * CHUNK, (c + 1) * CHUNK)
        for hd in range(N_HEADS):
            n = c * N_HEADS + hd
            sl = slice(hd * HEAD, (hd + 1) * HEAD)
            s_old = s_scr[hd]
            res = jnp.dot(mp_scr[n], s_old.astype(BF16), preferred_element_type=F32)
            o_scr[rows, sl] = res[HEAD:HEAD + CHUNK] + ob_scr[n]
            s_scr[hd] = (s_old * jnp.concatenate([egt_scr[n]] * (HEAD // V7X_SUBLANES), axis=0)
                         - res[0:HEAD] + sb_scr[n])
        tick()
    while side_work:
        tick()

    on = _head_rms_gate(o_scr[...], onw_ref[...], pr_scr[:, 0:d])
    yb = jnp.dot(on.astype(BF16), wob_ref[...], preferred_element_type=F32)
    merged = _sigmoid(pr_scr[:, d:2 * d]) * ya_scr[...] + _sigmoid(pr_scr[:, 2 * d:3 * d]) * yb
    post = _rms(jnp.dot(merged.astype(BF16), wo_ref[...], preferred_element_type=F32), npost_ref[...])
    y_ref[0] = x + gate_ref[0] * post

    @pl.when(t == pl.num_programs(1) - 1)
    def _():
        s_ref[0] = s_scr[...]


def _prompt(x, shift, scale, gate, consts):
    bsz, seq, d = x.shape
    dqkv = 3 * d
    lt = PROMPT_TILE
    pad = V7X_SUBLANES
    nb = (lt // CHUNK) * N_HEADS
    vm = pl.BlockSpec(memory_space=pltpu.VMEM)
    per_b = pl.BlockSpec((1, 1, d), lambda b, t: (b, 0, 0))
    return pl.pallas_call(
        _prompt_kernel,
        grid=(bsz, seq // lt),
        in_specs=[pl.BlockSpec((1, lt, d), lambda b, t: (b, t, 0)), per_b, per_b, per_b] + [vm] * len(consts),
        out_specs=[pl.BlockSpec((1, lt, d), lambda b, t: (b, t, 0)),
                   pl.BlockSpec((1, CONV_A_W - 1, d), lambda b, t: (b, 0, 0)),
                   pl.BlockSpec((1, CONV_B_W - 1, dqkv), lambda b, t: (b, 0, 0)),
                   pl.BlockSpec((1, N_HEADS, HEAD, HEAD), lambda b, t: (b, 0, 0, 0))],
        out_shape=[jax.ShapeDtypeStruct((bsz, seq, d), F32),
                   jax.ShapeDtypeStruct((bsz, CONV_A_W - 1, d), F32),
                   jax.ShapeDtypeStruct((bsz, CONV_B_W - 1, dqkv), F32),
                   jax.ShapeDtypeStruct((bsz, N_HEADS, HEAD, HEAD), F32)],
        scratch_shapes=[pltpu.VMEM((pad + lt, d), F32),
                        pltpu.VMEM((pad + lt, dqkv), F32),
                        pltpu.VMEM((N_HEADS, HEAD, HEAD), F32),
                        pltpu.VMEM((lt, d), F32), pltpu.VMEM((lt, d), F32),
                        pltpu.VMEM((lt, d), F32), pltpu.VMEM((lt, d), F32),
                        pltpu.VMEM((lt, V7X_LANES), F32), pltpu.VMEM((lt, V7X_LANES), F32),
                        pltpu.VMEM((lt, V7X_LANES), F32),
                        pltpu.VMEM((lt // CHUNK, 2 * N_HEADS, CHUNK), F32),
                        pltpu.VMEM((nb, CHUNK, 2 * CHUNK), F32),
                        pltpu.VMEM((nb, CHUNK, CHUNK), BF16), pltpu.VMEM((nb, CHUNK, 2 * HEAD), F32),
                        pltpu.VMEM((nb, CHUNK, HEAD), F32), pltpu.VMEM((nb, CHUNK, HEAD), BF16),
                        pltpu.VMEM((nb, V7X_SUBLANES, HEAD), F32), pltpu.VMEM((nb, HEAD + CHUNK, HEAD), BF16),
                        pltpu.VMEM((nb, HEAD, HEAD), F32), pltpu.VMEM((nb, CHUNK, HEAD), F32),
                        pltpu.VMEM((nb, CHUNK, 2 * HEAD), BF16),
                        pltpu.VMEM((lt, 4 * d), F32), pltpu.VMEM((lt, 3 * d), F32),
                        pltpu.VMEM((lt, d), BF16), pltpu.VMEM((lt, d), F32)],
        compiler_params=pltpu.CompilerParams(dimension_semantics=("arbitrary", "arbitrary"),
                                             vmem_limit_bytes=V7X_VMEM_LIMIT_BYTES),
        name="prompt",
    )(x, shift, scale, gate, *consts)


def _sample_front_kernel(x_ref, shift_ref, scale_ref, sa_ref, sb_ref, npre_ref, wmain_ref, wgate_ref, wsmall_ref,
                         caw_ref, cbw_ref, alc_ref, dtc_ref, woa_ref,
                         q_ref, k_ref, v_ref, be_ref, eg_ref, ya_ref, rest_ref, nsa_ref, nsb_ref):
    d = x_ref.shape[1]
    dqkv = 3 * d
    n = x_ref.shape[0]
    x = x_ref[...]
    h = _rms(x, npre_ref[...]) * (1.0 + scale_ref[...]) + shift_ref[...]
    hb = h.astype(BF16)

    pa = jnp.dot(hb, wmain_ref[:, 0:4 * d], preferred_element_type=F32)
    u = pa[:, d:2 * d] * pa[:, 0:d]
    conv = u * caw_ref[CONV_A_W - 1:CONV_A_W, :]
    for j in range(CONV_A_W - 1):
        conv = conv + sa_ref[j] * caw_ref[j:j + 1, :]
    ya_ref[...] = jnp.dot(((pa[:, 2 * d:3 * d] * conv) * _silu(pa[:, 3 * d:4 * d])).astype(BF16), woa_ref[...],
                          preferred_element_type=F32)
    for j in range(CONV_A_W - 2):
        nsa_ref[j] = sa_ref[j + 1]
    nsa_ref[CONV_A_W - 2] = u

    pq = jnp.dot(hb, wmain_ref[:, 4 * d:4 * d + dqkv], preferred_element_type=F32)
    cq = pq * cbw_ref[CONV_B_W - 1:CONV_B_W, :]
    for j in range(CONV_B_W - 1):
        cq = cq + sb_ref[j] * cbw_ref[j:j + 1, :]
    cq = _silu(cq)
    for j in range(CONV_B_W - 2):
        nsb_ref[j] = sb_ref[j + 1]
    nsb_ref[CONV_B_W - 2] = pq
    q_ref[...] = _head_l2norm(cq[:, 0:d], HEAD ** -0.5)
    k_ref[...] = _head_l2norm(cq[:, d:2 * d], 1.0)
    v_ref[...] = cq[:, 2 * d:3 * d]

    rest_ref[:, 0:d] = jnp.dot(hb, wmain_ref[:, 4 * d + dqkv:5 * d + dqkv], preferred_element_type=F32)
    rest_ref[:, d:3 * d] = jnp.dot(hb, wgate_ref[...], preferred_element_type=F32)

    ps = jnp.dot(hb, wsmall_ref[...], preferred_element_type=F32)
    beta = _sigmoid(ps)
    eg = jnp.exp(-jnp.exp(alc_ref[...]) * _softplus(ps + dtc_ref[...]))
    be_ref[...] = jnp.concatenate([_lane_bcast(beta, hd, HEAD) for hd in range(N_HEADS)], axis=-1)
    eg_ref[...] = jnp.concatenate([_lane_bcast(eg, N_HEADS + hd, HEAD) for hd in range(N_HEADS)], axis=-1)


def _sample_state_kernel(s_ref, q_ref, k_ref, v_ref, be_ref, eg_ref, so_ref, o_ref):
    bb = q_ref.shape[0]
    row = lax.broadcasted_iota(jnp.int32, (bb, HEAD), 0)
    erow = lax.broadcasted_iota(jnp.int32, (bb, bb * HEAD), 0)
    elane = lax.broadcasted_iota(jnp.int32, (bb, bb * HEAD), 1)
    diag = (elane // HEAD) == erow
    zeros_pad = jnp.zeros((2 * V7X_SUBLANES - bb, HEAD), F32)
    for hd in range(N_HEADS):
        sl = slice(hd * HEAD, (hd + 1) * HEAD)
        qh = q_ref[:, sl]
        kh = k_ref[:, sl]
        vh = v_ref[:, sl]
        eg = eg_ref[:, sl]
        kq = jnp.concatenate([kh, qh], axis=0).astype(BF16)
        ks = jnp.zeros((bb, HEAD), F32)
        qs = jnp.zeros((bb, HEAD), F32)
        for tk in range(bb):
            r = jnp.dot(kq, s_ref[tk, hd].astype(BF16), preferred_element_type=F32)
            ks = jnp.where(row == tk, r[0:bb], ks)
            qs = jnp.where(row == tk, r[bb:2 * bb], qs)
        vn = be_ref[:, sl] * (vh - eg * ks)
        o_ref[:, sl] = eg * qs + jnp.sum(qh * kh, axis=-1, keepdims=True) * vn
        vexp = jnp.where(diag, jnp.concatenate([vn] * bb, axis=-1), 0.0)
        outer = _dot_tn(jnp.concatenate([kh, zeros_pad], axis=0),
                        jnp.concatenate([vexp, jnp.zeros((2 * V7X_SUBLANES - bb, bb * HEAD), F32)], axis=0))
        for tk in range(bb):
            so_ref[tk, hd] = (s_ref[tk, hd] * jnp.broadcast_to(eg[tk:tk + 1, :], (HEAD, HEAD))
                              + outer[:, tk * HEAD:(tk + 1) * HEAD])


def _sample_back_kernel(x_ref, gate_ref, o_ref, ya_ref, rest_ref, onw_ref, wob_ref, wo_ref, npost_ref, y_ref):
    d = x_ref.shape[1]
    on = _head_rms_gate(o_ref[...], onw_ref[...], rest_ref[:, 0:d])
    yb = jnp.dot(on.astype(BF16), wob_ref[...], preferred_element_type=F32)
    merged = _sigmoid(rest_ref[:, d:2 * d]) * ya_ref[...] + _sigmoid(rest_ref[:, 2 * d:3 * d]) * yb
    post = _rms(jnp.dot(merged.astype(BF16), wo_ref[...], preferred_element_type=F32), npost_ref[...])
    y_ref[...] = x_ref[...] + gate_ref[...] * post


def _sample(x, shift, scale, gate, sa, sb, s0, w):
    n, d = x.shape
    dqkv = 3 * d
    vm = pl.BlockSpec(memory_space=pltpu.VMEM)
    params = pltpu.CompilerParams(vmem_limit_bytes=V7X_VMEM_LIMIT_BYTES)
    row = jax.ShapeDtypeStruct((n, d), F32)
    front_in = (x, shift, scale, sa, sb, w["npre"], w["wmain"], w["wgate"], w["wsmall"], w["caw"], w["cbw"],
                w["alc"], w["dtc"], w["woa"])
    q, k, v, be, eg, ya, rest, nsa, nsb = pl.pallas_call(
        _sample_front_kernel,
        in_specs=[vm] * len(front_in),
        out_specs=[vm] * 9,
        out_shape=[row, row, row, row, row, row, jax.ShapeDtypeStruct((n, 3 * d), F32),
                   jax.ShapeDtypeStruct(sa.shape, F32), jax.ShapeDtypeStruct(sb.shape, F32)],
        compiler_params=params,
        name="sample_front",
    )(*front_in)

    bb = SAMPLE_BLOCK
    tok = pl.BlockSpec((bb, d), lambda i: (i, 0))
    st = pl.BlockSpec((bb, N_HEADS, HEAD, HEAD), lambda i: (i, 0, 0, 0))
    s_new, o = pl.pallas_call(
        _sample_state_kernel,
        grid=(n // bb,),
        in_specs=[st, tok, tok, tok, tok, tok],
        out_specs=[st, tok],
        out_shape=[jax.ShapeDtypeStruct(s0.shape, F32), row],
        compiler_params=pltpu.CompilerParams(dimension_semantics=("arbitrary",),
                                             vmem_limit_bytes=V7X_VMEM_LIMIT_BYTES),
        name="sample_state",
    )(s0, q, k, v, be, eg)

    back_in = (x, gate, o, ya, rest, w["onw"], w["wob"], w["wo"], w["npost"])
    y = pl.pallas_call(
        _sample_back_kernel,
        in_specs=[vm] * len(back_in),
        out_specs=vm,
        out_shape=row,
        compiler_params=params,
        name="sample_back",
    )(*back_in)
    return y, nsa, nsb, s_new


def kernel(x_prompt, x_sample, c_prompt, c_sample, state_conv_a, state_conv_qkv, state_delta, ada_w, ada_b,
           norm_pre, w_in, conv_a_w, conv_b_w, a_log, dt_bias, onorm_w, w_out_a, w_out_b, w_o, norm_post):
    depth = w_in.shape[0]
    assert depth == 1, "single-layer trunk"
    bp, seq, d = x_prompt.shape
    ns = x_sample.shape[0]
    dqkv = 3 * d
    assert d == N_HEADS * HEAD and seq % PROMPT_TILE == 0 and PROMPT_TILE % CHUNK == 0 and 2 * CHUNK == V7X_LANES
    assert ns % SAMPLE_BLOCK == 0 and x_sample.shape[1] == 1

    off_small = 4 * d + dqkv + d
    w0 = w_in.reshape(w_in.shape[1:])
    wmain = w0.astype(BF16)
    wgate = wmain[:, off_small + 2 * N_HEADS:]
    wsmall = jnp.pad(wmain[:, off_small:off_small + 2 * N_HEADS], ((0, 0), (0, V7X_LANES - 2 * N_HEADS)))
    wsmallt = wmain[:, off_small:off_small + 2 * N_HEADS].T
    zeros_h = jnp.zeros((N_HEADS,), F32)
    lane_pad = jnp.zeros((V7X_LANES - 2 * N_HEADS,), F32)
    w = {
        "npre": norm_pre, "npost": norm_post, "onw": onorm_w,
        "wmain": wmain, "wgate": wgate, "wsmall": wsmall, "wsmallt": wsmallt,
        "caw": conv_a_w[0], "cbw": conv_b_w[0],
        "alc": jnp.concatenate([zeros_h, a_log[0], lane_pad])[None, :],
        "dtc": jnp.concatenate([zeros_h, dt_bias[0], lane_pad])[None, :],
        "alr": jnp.concatenate([zeros_h, a_log[0]])[:, None],
        "dtr": jnp.concatenate([zeros_h, dt_bias[0]])[:, None],
        "woa": w_out_a[0].astype(BF16), "wob": w_out_b[0].astype(BF16), "wo": w_o[0].astype(BF16),
    }

    mod = _adaln(jnp.concatenate([c_prompt, c_sample], axis=0), ada_w.reshape(ada_w.shape[1:]), ada_b)
    shift, scale, gate = mod[:, 0:d], mod[:, d:2 * d], mod[:, 2 * d:3 * d]

    consts = (w["npre"], w["wmain"], w["wgate"], w["wsmall"], w["wsmallt"], w["caw"], w["cbw"], w["alc"], w["dtc"],
              w["alr"], w["dtr"], w["onw"], w["woa"], w["wob"], w["wo"], w["npost"])
    yp, pca, pcb, pds = _prompt(x_prompt, shift[:bp, None, :], scale[:bp, None, :], gate[:bp, None, :], consts)

    ys, nsa, nsb, sds = _sample(
        x_sample[:, 0, :], shift[bp:], scale[bp:], gate[bp:],
        jnp.transpose(state_conv_a.reshape(ns, CONV_A_W - 1, d), (1, 0, 2)),
        jnp.transpose(state_conv_qkv.reshape(ns, CONV_B_W - 1, dqkv), (1, 0, 2)),
        state_delta.reshape(state_delta.shape[1:]), w)

    return (yp, ys.reshape(ns, 1, d), pca.reshape((1,) + pca.shape), pcb.reshape((1,) + pcb.shape),
            pds.reshape((1,) + pds.shape).astype(state_delta.dtype),
            jnp.transpose(nsa, (1, 0, 2)).reshape(1, ns, CONV_A_W - 1, d),
            jnp.transpose(nsb, (1, 0, 2)).reshape(1, ns, CONV_B_W - 1, dqkv),
            sds.reshape((1,) + sds.shape).astype(state_delta.dtype))
```

```python
import functools

import jax
import jax.numpy as jnp
from jax import lax
from jax.experimental import pallas as pl
from jax.experimental.pallas import tpu as pltpu

F32 = jnp.float32
BF16 = jnp.bfloat16

EPS = 1e-6
N_HEADS = 8
HEAD = 128
CONV_A_W = 3
CONV_B_W = 4

V7X_LANES = 128
V7X_SUBLANES = 8
V7X_VMEM_LIMIT_BYTES = 60 * 1024 * 1024

PROMPT_TILE = 256
SIDE_COLS = 512
SIDE_EVERY_PREP = 4
SIDE_EVERY_SOLVE = 32
SIDE_EVERY = 16
CHUNK = 64
SAMPLE_BLOCK = 8

HI = lax.Precision.HIGHEST


NEG_LOG2_E = -1.4426950408889634


def _sigmoid(x):
    return 1.0 / (1.0 + jnp.exp2(x * NEG_LOG2_E))


def _silu(x):
    return x * _sigmoid(x)


def _softplus(x):
    return jnp.maximum(x, 0.0) + jnp.log1p(jnp.exp(-jnp.abs(x)))


def _dot(a, b):
    return jnp.dot(a.astype(BF16), b.astype(BF16), preferred_element_type=F32)


def _dot_nt(a, b):
    return lax.dot_general(a.astype(BF16), b.astype(BF16), (((1,), (1,)), ((), ())),
                           preferred_element_type=F32)


def _dot_tn(a, b):
    return lax.dot_general(a.astype(BF16), b.astype(BF16), (((0,), (0,)), ((), ())),
                           preferred_element_type=F32)


def _dot_exact(a, b):
    return jnp.dot(a, b, precision=HI, preferred_element_type=F32)


def _split(a):
    hi = a.astype(BF16)
    return hi, (a - hi.astype(F32)).astype(BF16)


def _solve_dot(a, b):
    ah, al = _split(a)
    bh, bl = _split(b)
    return (jnp.dot(ah, bh, preferred_element_type=F32) + jnp.dot(ah, bl, preferred_element_type=F32)
            + jnp.dot(al, bh, preferred_element_type=F32))


def _rms(x, w):
    return x * lax.rsqrt(jnp.mean(x * x, axis=-1, keepdims=True) + EPS) * w


def _head_l2norm(x, scale):
    outs = []
    for h in range(x.shape[1] // HEAD):
        xh = x[:, h * HEAD:(h + 1) * HEAD]
        outs.append(xh * (lax.rsqrt(jnp.sum(xh * xh, axis=-1, keepdims=True) + EPS) * scale))
    return jnp.concatenate(outs, axis=-1)


def _head_rms_gate(o, w, z):
    outs = []
    for h in range(N_HEADS):
        sl = slice(h * HEAD, (h + 1) * HEAD)
        oh = o[:, sl]
        outs.append(oh * lax.rsqrt(jnp.mean(oh * oh, axis=-1, keepdims=True) + EPS) * w * _silu(z[:, sl]))
    return jnp.concatenate(outs, axis=-1)


def _causal_taps(buf_ref, cols, w_ref, width, rows, zero):
    sub = V7X_SUBLANES
    n_tiles = rows // sub
    tiles = [buf_ref[i * sub:(i + 1) * sub, cols] for i in range(n_tiles + 1)]
    width_cols = tiles[0].shape[1]
    row = lax.broadcasted_iota(jnp.int32, (sub, width_cols), 0)
    w_last = jnp.broadcast_to(w_ref[width - 1:width, cols], (sub, width_cols))
    acc = [tiles[i + 1] * w_last for i in range(n_tiles)]
    for j in range(width - 1):
        s = width - 1 - j
        w_j = jnp.broadcast_to(w_ref[j:j + 1, cols], (sub, width_cols))
        rots = [pltpu.roll(tl, s + zero, axis=0) for tl in tiles]
        from_above = row < s
        acc = [acc[i] + jnp.where(from_above, rots[i], rots[i + 1]) * w_j for i in range(n_tiles)]
    return jnp.concatenate(acc, axis=0)


def _lane_bcast(x, col, width):
    return jnp.broadcast_to(x[:, col:col + 1], (x.shape[0], width))


def _adaln_kernel(c_ref, w_ref, b_ref, o_ref):
    o_ref[...] = _dot(_silu(c_ref[...]), w_ref[...]) + b_ref[...]


def _adaln(c_all, ada_w, ada_b):
    n, d = c_all.shape
    cols = ada_w.shape[1]
    return pl.pallas_call(
        _adaln_kernel,
        grid=(cols // d,),
        in_specs=[pl.BlockSpec((n, d), lambda j: (0, 0)),
                  pl.BlockSpec((d, d), lambda j: (0, j)),
                  pl.BlockSpec((1, d), lambda j: (0, j))],
        out_specs=pl.BlockSpec((n, d), lambda j: (0, j)),
        out_shape=jax.ShapeDtypeStruct((n, cols), F32),
        name="adaln",
    )(c_all, ada_w, ada_b)


def _prompt_kernel(x_ref, shift_ref, scale_ref, gate_ref, npre_ref, wmain_ref, wgate_ref, wsmall_ref, wsmallt_ref,
                   caw_ref, cbw_ref, alc_ref, dtc_ref, alr_ref, dtr_ref, onw_ref, woa_ref, wob_ref, wo_ref,
                   npost_ref,
                   y_ref, ca_ref, cb_ref, s_ref,
                   abuf, qkvbuf, s_scr, q_scr, k_scr, v_scr, o_scr, b_scr, gc_scr, gt_scr, gcr_scr,
                   r_scr, aqk_scr, rhs_scr, qd_scr, kd_scr, egt_scr, mp_scr, sb_scr, ob_scr, wu_scr,
                   pa_scr, pr_scr, yap_scr, ya_scr):
    lt = x_ref.shape[1]
    d = x_ref.shape[2]
    dqkv = qkvbuf.shape[1]
    n_chunks = lt // CHUNK
    pad = V7X_SUBLANES
    t = pl.program_id(1)

    @pl.when(t == 0)
    def _():
        abuf[0:pad, :] = jnp.zeros((pad, d), F32)
        qkvbuf[0:pad, :] = jnp.zeros((pad, dqkv), F32)
        s_scr[...] = jnp.zeros(s_scr.shape, F32)

    x = x_ref[0]
    h = _rms(x, npre_ref[...]) * (1.0 + scale_ref[0]) + shift_ref[0]
    hb = h.astype(BF16)

    side_work = []

    def side_proj(dst_ref, dst_off, w_ref, w_off, blk):
        def run():
            dst_ref[:, dst_off + blk * SIDE_COLS:dst_off + (blk + 1) * SIDE_COLS] = jnp.dot(
                hb, w_ref[:, w_off + blk * SIDE_COLS:w_off + (blk + 1) * SIDE_COLS], preferred_element_type=F32)
        return run

    def side_branch_a(blk):
        def run():
            cols = slice(blk * SIDE_COLS, (blk + 1) * SIDE_COLS)
            u = pa_scr[:, d + blk * SIDE_COLS:d + (blk + 1) * SIDE_COLS] * pa_scr[:, cols]
            abuf[pad:pad + lt, cols] = u
            conv = _causal_taps(abuf, cols, caw_ref, CONV_A_W, lt, jnp.minimum(t, 0))
            b_a = pa_scr[:, 2 * d + blk * SIDE_COLS:2 * d + (blk + 1) * SIDE_COLS]
            z_a = pa_scr[:, 3 * d + blk * SIDE_COLS:3 * d + (blk + 1) * SIDE_COLS]
            yap_scr[:, cols] = ((b_a * conv) * _silu(z_a)).astype(BF16)
            ca_ref[0, :, cols] = abuf[pad + lt - (CONV_A_W - 1):pad + lt, cols]
            abuf[0:pad, cols] = abuf[lt:lt + pad, cols]
        return run

    def side_out_a(blk):
        def run():
            cols = slice(blk * SIDE_COLS, (blk + 1) * SIDE_COLS)
            ya_scr[:, cols] = jnp.dot(yap_scr[...], woa_ref[:, cols], preferred_element_type=F32)
        return run

    side_work += [side_proj(pa_scr, 0, wmain_ref, 0, blk) for blk in range(4 * d // SIDE_COLS)]
    side_work += [side_proj(pr_scr, 0, wmain_ref, 4 * d + dqkv, blk) for blk in range(d // SIDE_COLS)]
    side_work += [side_proj(pr_scr, d, wgate_ref, 0, blk) for blk in range(2 * d // SIDE_COLS)]
    side_work += [side_branch_a(blk) for blk in range(d // SIDE_COLS)]
    side_work += [side_out_a(blk) for blk in range(d // SIDE_COLS)]
    side_work.reverse()

    def tick():
        if side_work:
            side_work.pop()()

    def conv_qkv(blk):
        cols = slice(blk * SIDE_COLS, (blk + 1) * SIDE_COLS)
        cq = _silu(_causal_taps(qkvbuf, cols, cbw_ref, CONV_B_W, lt, jnp.minimum(t, 0)))
        cb_ref[0, :, cols] = qkvbuf[pad + lt - (CONV_B_W - 1):pad + lt, cols]
        qkvbuf[0:pad, cols] = qkvbuf[lt:lt + pad, cols]
        dst = slice((blk * SIDE_COLS) % d, (blk * SIDE_COLS) % d + SIDE_COLS)
        if blk * SIDE_COLS < d:
            q_scr[:, dst] = _head_l2norm(cq, HEAD ** -0.5)
        elif blk * SIDE_COLS < 2 * d:
            k_scr[:, dst] = _head_l2norm(cq, 1.0)
        else:
            v_scr[:, dst] = cq

    def proj_qkv(blk):
        cols = slice(blk * SIDE_COLS, (blk + 1) * SIDE_COLS)
        qkvbuf[pad:pad + lt, cols] = jnp.dot(hb, wmain_ref[:, 4 * d + blk * SIDE_COLS:4 * d + (blk + 1) * SIDE_COLS],
                                             preferred_element_type=F32)

    proj_qkv(0)
    for blk in range(dqkv // SIDE_COLS):
        if blk + 1 < dqkv // SIDE_COLS:
            proj_qkv(blk + 1)
        else:
            tick()
        conv_qkv(blk)

    ps = jnp.dot(hb, wsmall_ref[...], preferred_element_type=F32)
    pst = lax.dot_general(wsmallt_ref[...], hb, (((1,), (1,)), ((), ())), preferred_element_type=F32)
    b_scr[...] = _sigmoid(ps)
    g_col = -jnp.exp(alc_ref[...]) * _softplus(ps + dtc_ref[...])
    g_row = -jnp.exp(alr_ref[...]) * _softplus(pst + dtr_ref[...])
    row_in_chunk = lax.broadcasted_iota(jnp.int32, (lt, V7X_LANES), 0) % CHUNK
    gc_col = g_col
    shift = 1
    while shift < CHUNK:
        gc_col = gc_col + jnp.where(row_in_chunk >= shift, pltpu.roll(gc_col, shift, axis=0), 0.0)
        shift *= 2
    gc_scr[...] = gc_col
    gt_scr[...] = jnp.concatenate(
        [jnp.broadcast_to(gc_col[(c + 1) * CHUNK - 1:(c + 1) * CHUNK, :], (CHUNK, V7X_LANES))
         for c in range(n_chunks)], axis=0)
    rj = lax.broadcasted_iota(jnp.int32, (CHUNK, CHUNK), 0)
    cj = lax.broadcasted_iota(jnp.int32, (CHUNK, CHUNK), 1)
    triu = jnp.where(rj <= cj, 1.0, 0.0).astype(F32)
    for c in range(n_chunks):
        gcr_scr[c] = _dot_exact(g_row[:, c * CHUNK:(c + 1) * CHUNK], triu)
    tick()

    causal = rj >= cj
    strict = rj > cj
    eye = jnp.where(rj == cj, 1.0, 0.0).astype(F32)

    nb = n_chunks * N_HEADS
    lane = lax.broadcasted_iota(jnp.int32, (CHUNK, 2 * CHUNK), 1)
    low_half = lane < CHUNK
    eye_left = jnp.where(lane == lax.broadcasted_iota(jnp.int32, (CHUNK, 2 * CHUNK), 0), 1.0, 0.0).astype(BF16)

    for c in range(n_chunks):
        rows = slice(c * CHUNK, (c + 1) * CHUNK)
        beta_blk = b_scr[rows, :]
        gc_blk = gc_scr[rows, :]
        gt_blk = gt_scr[rows, :]
        gcr_blk = gcr_scr[c]
        for hd in range(N_HEADS):
            n = c * N_HEADS + hd
            sl = slice(hd * HEAD, (hd + 1) * HEAD)
            qh = q_scr[rows, sl]
            kh = k_scr[rows, sl]
            vh = v_scr[rows, sl]
            bcol = _lane_bcast(beta_blk, hd, HEAD)
            gcol = _lane_bcast(gc_blk, N_HEADS + hd, HEAD)
            gtot = _lane_bcast(gt_blk, N_HEADS + hd, HEAD)
            grow = gcr_blk[N_HEADS + hd:N_HEADS + hd + 1, :]
            decay = jnp.exp(jnp.where(causal, gcol[:, 0:CHUNK] - grow, -jnp.inf))
            egc = jnp.exp(gcol)
            kb = kh * bcol
            kq = _dot_nt(jnp.concatenate([kb, qh], axis=0), kh)
            xm = -jnp.where(strict, kq[0:CHUNK] * decay, 0.0)
            r_scr[n] = jnp.concatenate([xm, eye], axis=-1)
            aqk_scr[n] = (kq[CHUNK:2 * CHUNK] * decay).astype(BF16)
            rhs_scr[n] = jnp.concatenate([vh * bcol, kb * egc], axis=-1)
            qd_scr[n] = qh * egc
            kd_scr[n] = (kh * jnp.exp(gtot - gcol)).astype(BF16)
            egt_scr[n] = jnp.exp(gtot[0:V7X_SUBLANES, :])
            if n % SIDE_EVERY_PREP == SIDE_EVERY_PREP - 1:
                tick()

    def split_parts(r):
        hi = r.astype(BF16)
        hi32 = hi.astype(F32)
        lo32 = r - hi32
        return hi, hi32, lo32

    m = 1
    while m < CHUNK:
        for n in range(nb):
            r = r_scr[n]
            hi, hi32, lo32 = split_parts(r)
            lhs = jnp.where(low_half, hi32, pltpu.roll(lo32, CHUNK, axis=1)).astype(BF16)
            lo = lo32.astype(BF16)
            him = hi - eye_left
            prod = jnp.dot(jnp.concatenate([lhs, lhs], axis=1), jnp.concatenate([him, him, lo, lo], axis=0),
                           preferred_element_type=F32)
            r_scr[n] = prod + r
            if n % SIDE_EVERY_SOLVE == SIDE_EVERY_SOLVE - 1:
                tick()
        m *= 2
    for n in range(nb):
        hi, hi32, lo32 = split_parts(r_scr[n])
        lhs = jnp.where(low_half, pltpu.roll(lo32, CHUNK, axis=1), hi32).astype(BF16)
        bh, _, bl32 = split_parts(rhs_scr[n])
        bl = bl32.astype(BF16)
        sol = jnp.dot(jnp.concatenate([lhs, lhs], axis=1), jnp.concatenate([bh, bh, bl, bl], axis=0),
                      preferred_element_type=F32)
        wu_scr[n] = jnp.concatenate([sol[:, HEAD:2 * HEAD], sol[:, 0:HEAD]], axis=1).astype(BF16)
        if n % SIDE_EVERY == SIDE_EVERY - 1:
            tick()
    for n in range(nb):
        wu = wu_scr[n]
        kw = lax.dot_general(kd_scr[n], wu, (((0,), (0,)), ((), ())), preferred_element_type=F32)
        aw = jnp.dot(aqk_scr[n], wu, preferred_element_type=F32)
        mp_scr[n, 0:HEAD, :] = kw[:, 0:HEAD].astype(BF16)
        mp_scr[n, HEAD:HEAD + CHUNK, :] = (qd_scr[n] - aw[:, 0:HEAD]).astype(BF16)
        sb_scr[n] = kw[:, HEAD:2 * HEAD]
        ob_scr[n] = aw[:, HEAD:2 * HEAD]
        if n % SIDE_EVERY == SIDE_EVERY - 1:
            tick()

    zero_blk = jnp.zeros((HEAD, HEAD), BF16)
    for c in range(n_chunks):
        rows = slice(c * CHUNK, (c + 1) * CHUNK)
        for hd in range(0, N_HEADS, 2):
            n = c * N_HEADS + hd
            s_pair = [s_scr[hd], s_scr[hd + 1]]
            state_diag = jnp.concatenate(
                [jnp.concatenate([s_pair[0].astype(BF16), zero_blk], axis=1),
                 jnp.concatenate([zero_blk, s_pair[1].astype(BF16)], axis=1)], axis=0)
            res2 = jnp.dot(jnp.concatenate([mp_scr[n], mp_scr[n + 1]], axis=1), state_diag,
                           preferred_element_type=F32)
            for i in range(2):
                res = res2[:, i * HEAD:(i + 1) * HEAD]
                sl = slice((hd + i) * HEAD, (hd + i + 1) * HEAD)
                o_scr[rows, sl] = res[HEAD:HEAD + CHUNK] + ob_scr[n + i]
                s_scr[hd + i] = (s_pair[i] * jnp.concatenate([egt_scr[n + i]] * (HEAD // V7X_SUBLANES), axis=0)
                                 - res[0:HEAD] + sb_scr[n + i])
        tick()
    while side_work:
        tick()

    on = _head_rms_gate(o_scr[...], onw_ref[...], pr_scr[:, 0:d])
    yb = jnp.dot(on.astype(BF16), wob_ref[...], preferred_element_type=F32)
    merged = _sigmoid(pr_scr[:, d:2 * d]) * ya_scr[...] + _sigmoid(pr_scr[:, 2 * d:3 * d]) * yb
    post = _rms(jnp.dot(merged.astype(BF16), wo_ref[...], preferred_element_type=F32), npost_ref[...])
    y_ref[0] = x + gate_ref[0] * post

    @pl.when(t == pl.num_programs(1) - 1)
    def _():
        s_ref[0] = s_scr[...]


def _prompt(x, shift, scale, gate, consts):
    bsz, seq, d = x.shape
    dqkv = 3 * d
    lt = PROMPT_TILE
    pad = V7X_SUBLANES
    nb = (lt // CHUNK) * N_HEADS
    vm = pl.BlockSpec(memory_space=pltpu.VMEM)
    per_b = pl.BlockSpec((1, 1, d), lambda b, t: (b, 0, 0))
    return pl.pallas_call(
        _prompt_kernel,
        grid=(bsz, seq // lt),
        in_specs=[pl.BlockSpec((1, lt, d), lambda b, t: (b, t, 0)), per_b, per_b, per_b] + [vm] * len(consts),
        out_specs=[pl.BlockSpec((1, lt, d), lambda b, t: (b, t, 0)),
                   pl.BlockSpec((1, CONV_A_W - 1, d), lambda b, t: (b, 0, 0)),
                   pl.BlockSpec((1, CONV_B_W - 1, dqkv), lambda b, t: (b, 0, 0)),
                   pl.BlockSpec((1, N_HEADS, HEAD, HEAD), lambda b, t: (b, 0, 0, 0))],
        out_shape=[jax.ShapeDtypeStruct((bsz, seq, d), F32),
                   jax.ShapeDtypeStruct((bsz, CONV_A_W - 1, d), F32),
                   jax.ShapeDtypeStruct((bsz, CONV_B_W - 1, dqkv), F32),
                   jax.ShapeDtypeStruct((bsz, N_HEADS, HEAD, HEAD), F32)],
        scratch_shapes=[pltpu.VMEM((pad + lt, d), F32),
                        pltpu.VMEM((pad + lt, dqkv), F32),
                        pltpu.VMEM((N_HEADS, HEAD, HEAD), F32),
                        pltpu.VMEM((lt, d), F32), pltpu.VMEM((lt, d), F32),
                        pltpu.VMEM((lt, d), F32), pltpu.VMEM((lt, d), F32),
                        pltpu.VMEM((lt, V7X_LANES), F32), pltpu.VMEM((lt, V7X_LANES), F32),
                        pltpu.VMEM((lt, V7X_LANES), F32),
                        pltpu.VMEM((lt // CHUNK, 2 * N_HEADS, CHUNK), F32),
                        pltpu.VMEM((nb, CHUNK, 2 * CHUNK), F32),
                        pltpu.VMEM((nb, CHUNK, CHUNK), BF16), pltpu.VMEM((nb, CHUNK, 2 * HEAD), F32),
                        pltpu.VMEM((nb, CHUNK, HEAD), F32), pltpu.VMEM((nb, CHUNK, HEAD), BF16),
                        pltpu.VMEM((nb, V7X_SUBLANES, HEAD), F32), pltpu.VMEM((nb, HEAD + CHUNK, HEAD), BF16),
                        pltpu.VMEM((nb, HEAD, HEAD), F32), pltpu.VMEM((nb, CHUNK, HEAD), F32),
                        pltpu.VMEM((nb, CHUNK, 2 * HEAD), BF16),
                        pltpu.VMEM((lt, 4 * d), F32), pltpu.VMEM((lt, 3 * d), F32),
                        pltpu.VMEM((lt, d), BF16), pltpu.VMEM((lt, d), F32)],
        compiler_params=pltpu.CompilerParams(dimension_semantics=("arbitrary", "arbitrary"),
                                             vmem_limit_bytes=V7X_VMEM_LIMIT_BYTES),
        name="prompt",
    )(x, shift, scale, gate, *consts)


def _sample_front_kernel(x_ref, shift_ref, scale_ref, sa_ref, sb_ref, npre_ref, wmain_ref, wgate_ref, wsmall_ref,
                         caw_ref, cbw_ref, alc_ref, dtc_ref, woa_ref,
                         q_ref, k_ref, v_ref, be_ref, eg_ref, ya_ref, rest_ref, nsa_ref, nsb_ref):
    d = x_ref.shape[1]
    dqkv = 3 * d
    n = x_ref.shape[0]
    x = x_ref[...]
    h = _rms(x, npre_ref[...]) * (1.0 + scale_ref[...]) + shift_ref[...]
    hb = h.astype(BF16)

    pa = jnp.dot(hb, wmain_ref[:, 0:4 * d], preferred_element_type=F32)
    u = pa[:, d:2 * d] * pa[:, 0:d]
    conv = u * caw_ref[CONV_A_W - 1:CONV_A_W, :]
    for j in range(CONV_A_W - 1):
        conv = conv + sa_ref[j] * caw_ref[j:j + 1, :]
    ya_ref[...] = jnp.dot(((pa[:, 2 * d:3 * d] * conv) * _silu(pa[:, 3 * d:4 * d])).astype(BF16), woa_ref[...],
                          preferred_element_type=F32)
    for j in range(CONV_A_W - 2):
        nsa_ref[j] = sa_ref[j + 1]
    nsa_ref[CONV_A_W - 2] = u

    pq = jnp.dot(hb, wmain_ref[:, 4 * d:4 * d + dqkv], preferred_element_type=F32)
    cq = pq * cbw_ref[CONV_B_W - 1:CONV_B_W, :]
    for j in range(CONV_B_W - 1):
        cq = cq + sb_ref[j] * cbw_ref[j:j + 1, :]
    cq = _silu(cq)
    for j in range(CONV_B_W - 2):
        nsb_ref[j] = sb_ref[j + 1]
    nsb_ref[CONV_B_W - 2] = pq
    q_ref[...] = _head_l2norm(cq[:, 0:d], HEAD ** -0.5)
    k_ref[...] = _head_l2norm(cq[:, d:2 * d], 1.0)
    v_ref[...] = cq[:, 2 * d:3 * d]

    rest_ref[:, 0:d] = jnp.dot(hb, wmain_ref[:, 4 * d + dqkv:5 * d + dqkv], preferred_element_type=F32)
    rest_ref[:, d:3 * d] = jnp.dot(hb, wgate_ref[...], preferred_element_type=F32)

    ps = jnp.dot(hb, wsmall_ref[...], preferred_element_type=F32)
    beta = _sigmoid(ps)
    eg = jnp.exp(-jnp.exp(alc_ref[...]) * _softplus(ps + dtc_ref[...]))
    be_ref[...] = jnp.concatenate([_lane_bcast(beta, hd, HEAD) for hd in range(N_HEADS)], axis=-1)
    eg_ref[...] = jnp.concatenate([_lane_bcast(eg, N_HEADS + hd, HEAD) for hd in range(N_HEADS)], axis=-1)


def _sample_state_kernel(s_ref, q_ref, k_ref, v_ref, be_ref, eg_ref, so_ref, o_ref):
    bb = q_ref.shape[0]
    row = lax.broadcasted_iota(jnp.int32, (bb, HEAD), 0)
    erow = lax.broadcasted_iota(jnp.int32, (bb, bb * HEAD), 0)
    elane = lax.broadcasted_iota(jnp.int32, (bb, bb * HEAD), 1)
    diag = (elane // HEAD) == erow
    zeros_pad = jnp.zeros((2 * V7X_SUBLANES - bb, HEAD), F32)
    heads = [slice(hd * HEAD, (hd + 1) * HEAD) for hd in range(N_HEADS)]
    ks_all, qs_all = [], []
    for sl in heads:
        kq = jnp.concatenate([k_ref[:, sl], q_ref[:, sl]], axis=0).astype(BF16)
        hd = sl.start // HEAD
        prods = [jnp.dot(kq, s_ref[tk, hd].astype(BF16), preferred_element_type=F32) for tk in range(bb)]
        ks = prods[0][0:bb]
        qs = prods[0][bb:2 * bb]
        for tk in range(1, bb):
            ks = jnp.where(row == tk, prods[tk][0:bb], ks)
            qs = jnp.where(row == tk, prods[tk][bb:2 * bb], qs)
        ks_all.append(ks)
        qs_all.append(qs)
    outers = []
    for sl, ks, qs in zip(heads, ks_all, qs_all):
        qh = q_ref[:, sl]
        kh = k_ref[:, sl]
        eg = eg_ref[:, sl]
        vn = be_ref[:, sl] * (v_ref[:, sl] - eg * ks)
        o_ref[:, sl] = eg * qs + jnp.sum(qh * kh, axis=-1, keepdims=True) * vn
        vexp = jnp.where(diag, jnp.concatenate([vn] * bb, axis=-1), 0.0)
        outers.append(_dot_tn(jnp.concatenate([kh, zeros_pad], axis=0),
                              jnp.concatenate([vexp, jnp.zeros((2 * V7X_SUBLANES - bb, bb * HEAD), F32)], axis=0)))
    for sl, outer in zip(heads, outers):
        hd = sl.start // HEAD
        eg = eg_ref[:, sl]
        for tk in range(bb):
            so_ref[tk, hd] = (s_ref[tk, hd] * jnp.broadcast_to(eg[tk:tk + 1, :], (HEAD, HEAD))
                              + outer[:, tk * HEAD:(tk + 1) * HEAD])


def _sample_back_kernel(x_ref, gate_ref, o_ref, ya_ref, rest_ref, onw_ref, wob_ref, wo_ref, npost_ref, y_ref):
    d = x_ref.shape[1]
    on = _head_rms_gate(o_ref[...], onw_ref[...], rest_ref[:, 0:d])
    yb = jnp.dot(on.astype(BF16), wob_ref[...], preferred_element_type=F32)
    merged = _sigmoid(rest_ref[:, d:2 * d]) * ya_ref[...] + _sigmoid(rest_ref[:, 2 * d:3 * d]) * yb
    post = _rms(jnp.dot(merged.astype(BF16), wo_ref[...], preferred_element_type=F32), npost_ref[...])
    y_ref[...] = x_ref[...] + gate_ref[...] * post


def _sample(x, shift, scale, gate, sa, sb, s0, w):
    n, d = x.shape
    dqkv = 3 * d
    vm = pl.BlockSpec(memory_space=pltpu.VMEM)
    params = pltpu.CompilerParams(vmem_limit_bytes=V7X_VMEM_LIMIT_BYTES)
    row = jax.ShapeDtypeStruct((n, d), F32)
    front_in = (x, shift, scale, sa, sb, w["npre"], w["wmain"], w["wgate"], w["wsmall"], w["caw"], w["cbw"],
                w["alc"], w["dtc"], w["woa"])
    q, k, v, be, eg, ya, rest, nsa, nsb = pl.pallas_call(
        _sample_front_kernel,
        in_specs=[vm] * len(front_in),
        out_specs=[vm] * 9,
        out_shape=[row, row, row, row, row, row, jax.ShapeDtypeStruct((n, 3 * d), F32),
                   jax.ShapeDtypeStruct(sa.shape, F32), jax.ShapeDtypeStruct(sb.shape, F32)],
        compiler_params=params,
        name="sample_front",
    )(*front_in)

    bb = SAMPLE_BLOCK
    tok = pl.BlockSpec((bb, d), lambda i: (i, 0))
    st = pl.BlockSpec((bb, N_HEADS, HEAD, HEAD), lambda i: (i, 0, 0, 0))
    s_new, o = pl.pallas_call(
        _sample_state_kernel,
        grid=(n // bb,),
        in_specs=[st, tok, tok, tok, tok, tok],
        out_specs=[st, tok],
        out_shape=[jax.ShapeDtypeStruct(s0.shape, F32), row],
        compiler_params=pltpu.CompilerParams(dimension_semantics=("arbitrary",),
                                             vmem_limit_bytes=V7X_VMEM_LIMIT_BYTES),
        name="sample_state",
    )(s0, q, k, v, be, eg)

    back_in = (x, gate, o, ya, rest, w["onw"], w["wob"], w["wo"], w["npost"])
    y = pl.pallas_call(
        _sample_back_kernel,
        in_specs=[vm] * len(back_in),
        out_specs=vm,
        out_shape=row,
        compiler_params=params,
        name="sample_back",
    )(*back_in)
    return y, nsa, nsb, s_new


def kernel(x_prompt, x_sample, c_prompt, c_sample, state_conv_a, state_conv_qkv, state_delta, ada_w, ada_b,
           norm_pre, w_in, conv_a_w, conv_b_w, a_log, dt_bias, onorm_w, w_out_a, w_out_b, w_o, norm_post):
    depth = w_in.shape[0]
    assert depth == 1, "single-layer trunk"
    bp, seq, d = x_prompt.shape
    ns = x_sample.shape[0]
    dqkv = 3 * d
    assert d == N_HEADS * HEAD and seq % PROMPT_TILE == 0 and PROMPT_TILE % CHUNK == 0 and 2 * CHUNK == V7X_LANES
    assert ns % SAMPLE_BLOCK == 0 and x_sample.shape[1] == 1

    off_small = 4 * d + dqkv + d
    w0 = w_in.reshape(w_in.shape[1:])
    wmain = w0.astype(BF16)
    wgate = wmain[:, off_small + 2 * N_HEADS:]
    wsmall = jnp.pad(wmain[:, off_small:off_small + 2 * N_HEADS], ((0, 0), (0, V7X_LANES - 2 * N_HEADS)))
    wsmallt = wmain[:, off_small:off_small + 2 * N_HEADS].T
    zeros_h = jnp.zeros((N_HEADS,), F32)
    lane_pad = jnp.zeros((V7X_LANES - 2 * N_HEADS,), F32)
    w = {
        "npre": norm_pre, "npost": norm_post, "onw": onorm_w,
        "wmain": wmain, "wgate": wgate, "wsmall": wsmall, "wsmallt": wsmallt,
        "caw": conv_a_w[0], "cbw": conv_b_w[0],
        "alc": jnp.concatenate([zeros_h, a_log[0], lane_pad])[None, :],
        "dtc": jnp.concatenate([zeros_h, dt_bias[0], lane_pad])[None, :],
        "alr": jnp.concatenate([zeros_h, a_log[0]])[:, None],
        "dtr": jnp.concatenate([zeros_h, dt_bias[0]])[:, None],
        "woa": w_out_a[0].astype(BF16), "wob": w_out_b[0].astype(BF16), "wo": w_o[0].astype(BF16),
    }

    mod = _adaln(jnp.concatenate([c_prompt, c_sample], axis=0), ada_w.reshape(ada_w.shape[1:]), ada_b)
    shift, scale, gate = mod[:, 0:d], mod[:, d:2 * d], mod[:, 2 * d:3 * d]

    consts = (w["npre"], w["wmain"], w["wgate"], w["wsmall"], w["wsmallt"], w["caw"], w["cbw"], w["alc"], w["dtc"],
              w["alr"], w["dtr"], w["onw"], w["woa"], w["wob"], w["wo"], w["npost"])
    yp, pca, pcb, pds = _prompt(x_prompt, shift[:bp, None, :], scale[:bp, None, :], gate[:bp, None, :], consts)

    ys, nsa, nsb, sds = _sample(
        x_sample[:, 0, :], shift[bp:], scale[bp:], gate[bp:],
        jnp.transpose(state_conv_a.reshape(ns, CONV_A_W - 1, d), (1, 0, 2)),
        jnp.transpose(state_conv_qkv.reshape(ns, CONV_B_W - 1, dqkv), (1, 0, 2)),
        state_delta.reshape(state_delta.shape[1:]), w)

    return (yp, ys.reshape(ns, 1, d), pca.reshape((1,) + pca.shape), pcb.reshape((1,) + pcb.shape),
            pds.reshape((1,) + pds.shape).astype(state_delta.dtype),
            jnp.transpose(nsa, (1, 0, 2)).reshape(1, ns, CONV_A_W - 1, d),
            jnp.transpose(nsb, (1, 0, 2)).reshape(1, ns, CONV_B_W - 1, dqkv),
            sds.reshape((1,) + sds.shape).astype(state_delta.dtype))
```

```python
import functools

import jax
import jax.numpy as jnp
from jax import lax
from jax.experimental import pallas as pl
from jax.experimental.pallas import tpu as pltpu

F32 = jnp.float32
BF16 = jnp.bfloat16

EPS = 1e-6
N_HEADS = 8
HEAD = 128
CONV_A_W = 3
CONV_B_W = 4

V7X_LANES = 128
V7X_SUBLANES = 8
V7X_VMEM_LIMIT_BYTES = 60 * 1024 * 1024

PROMPT_TILE = 256
SIDE_COLS = 512
SIDE_EVERY_PREP = 4
SIDE_EVERY_SOLVE = 32
SIDE_EVERY = 16
CHUNK = 64
SAMPLE_BLOCK = 8

HI = lax.Precision.HIGHEST


NEG_LOG2_E = -1.4426950408889634


def _sigmoid(x):
    return 1.0 / (1.0 + jnp.exp2(x * NEG_LOG2_E))


def _silu(x):
    return x * _sigmoid(x)


def _softplus(x):
    return jnp.maximum(x, 0.0) + jnp.log1p(jnp.exp(-jnp.abs(x)))


def _dot(a, b):
    return jnp.dot(a.astype(BF16), b.astype(BF16), preferred_element_type=F32)


def _dot_nt(a, b):
    return lax.dot_general(a.astype(BF16), b.astype(BF16), (((1,), (1,)), ((), ())),
                           preferred_element_type=F32)


def _dot_tn(a, b):
    return lax.dot_general(a.astype(BF16), b.astype(BF16), (((0,), (0,)), ((), ())),
                           preferred_element_type=F32)


def _dot_exact(a, b):
    return jnp.dot(a, b, precision=HI, preferred_element_type=F32)


def _rms(x, w):
    return x * lax.rsqrt(jnp.mean(x * x, axis=-1, keepdims=True) + EPS) * w


def _head_l2norm(x, scale):
    outs = []
    for h in range(x.shape[1] // HEAD):
        xh = x[:, h * HEAD:(h + 1) * HEAD]
        outs.append(xh * (lax.rsqrt(jnp.sum(xh * xh, axis=-1, keepdims=True) + EPS) * scale))
    return jnp.concatenate(outs, axis=-1)


def _head_rms_gate(o, w, z):
    outs = []
    for h in range(N_HEADS):
        sl = slice(h * HEAD, (h + 1) * HEAD)
        oh = o[:, sl]
        outs.append(oh * lax.rsqrt(jnp.mean(oh * oh, axis=-1, keepdims=True) + EPS) * w * _silu(z[:, sl]))
    return jnp.concatenate(outs, axis=-1)


def _causal_taps(buf_ref, cols, w_ref, width, rows, zero):
    sub = V7X_SUBLANES
    n_tiles = rows // sub
    tiles = [buf_ref[i * sub:(i + 1) * sub, cols] for i in range(n_tiles + 1)]
    width_cols = tiles[0].shape[1]
    row = lax.broadcasted_iota(jnp.int32, (sub, width_cols), 0)
    w_last = jnp.broadcast_to(w_ref[width - 1:width, cols], (sub, width_cols))
    acc = [tiles[i + 1] * w_last for i in range(n_tiles)]
    for j in range(width - 1):
        s = width - 1 - j
        w_j = jnp.broadcast_to(w_ref[j:j + 1, cols], (sub, width_cols))
        rots = [pltpu.roll(tl, s + zero, axis=0) for tl in tiles]
        from_above = row < s
        acc = [acc[i] + jnp.where(from_above, rots[i], rots[i + 1]) * w_j for i in range(n_tiles)]
    return jnp.concatenate(acc, axis=0)


def _lane_bcast(x, col, width):
    return jnp.broadcast_to(x[:, col:col + 1], (x.shape[0], width))


def _adaln_kernel(c_ref, w_ref, b_ref, o_ref):
    o_ref[...] = _dot(_silu(c_ref[...]), w_ref[...]) + b_ref[...]


def _adaln(c_all, ada_w, ada_b):
    n, d = c_all.shape
    cols = ada_w.shape[1]
    return pl.pallas_call(
        _adaln_kernel,
        grid=(cols // d,),
        in_specs=[pl.BlockSpec((n, d), lambda j: (0, 0)),
                  pl.BlockSpec((d, d), lambda j: (0, j)),
                  pl.BlockSpec((1, d), lambda j: (0, j))],
        out_specs=pl.BlockSpec((n, d), lambda j: (0, j)),
        out_shape=jax.ShapeDtypeStruct((n, cols), F32),
        name="adaln",
    )(c_all, ada_w, ada_b)


def _prompt_kernel(x_ref, shift_ref, scale_ref, gate_ref, npre_ref, wmain_ref, wgate_ref, wsmall_ref, wsmallt_ref,
                   caw_ref, cbw_ref, alc_ref, dtc_ref, alr_ref, dtr_ref, onw_ref, woa_ref, wob_ref, wo_ref,
                   npost_ref,
                   y_ref, ca_ref, cb_ref, s_ref,
                   abuf, qkvbuf, s_scr, q_scr, k_scr, v_scr, o_scr, b_scr, gc_scr, gt_scr, gcr_scr,
                   r_scr, aqk_scr, rhs_scr, qd_scr, kd_scr, egt_scr, mp_scr, sb_scr, ob_scr, wu_scr,
                   pa_scr, pr_scr, yap_scr, ya_scr):
    lt = x_ref.shape[1]
    d = x_ref.shape[2]
    dqkv = qkvbuf.shape[1]
    n_chunks = lt // CHUNK
    pad = V7X_SUBLANES
    t = pl.program_id(1)

    @pl.when(t == 0)
    def _():
        abuf[0:pad, :] = jnp.zeros((pad, d), F32)
        qkvbuf[0:pad, :] = jnp.zeros((pad, dqkv), F32)
        s_scr[...] = jnp.zeros(s_scr.shape, F32)

    x = x_ref[0]
    h = _rms(x, npre_ref[...]) * (1.0 + scale_ref[0]) + shift_ref[0]
    hb = h.astype(BF16)

    side_work = []

    def side_proj(dst_ref, dst_off, w_ref, w_off, blk):
        def run():
            dst_ref[:, dst_off + blk * SIDE_COLS:dst_off + (blk + 1) * SIDE_COLS] = jnp.dot(
                hb, w_ref[:, w_off + blk * SIDE_COLS:w_off + (blk + 1) * SIDE_COLS], preferred_element_type=F32)
        return run

    def side_branch_a(blk):
        def run():
            cols = slice(blk * SIDE_COLS, (blk + 1) * SIDE_COLS)
            u = pa_scr[:, d + blk * SIDE_COLS:d + (blk + 1) * SIDE_COLS] * pa_scr[:, cols]
            abuf[pad:pad + lt, cols] = u
            conv = _causal_taps(abuf, cols, caw_ref, CONV_A_W, lt, jnp.minimum(t, 0))
            b_a = pa_scr[:, 2 * d + blk * SIDE_COLS:2 * d + (blk + 1) * SIDE_COLS]
            z_a = pa_scr[:, 3 * d + blk * SIDE_COLS:3 * d + (blk + 1) * SIDE_COLS]
            yap_scr[:, cols] = ((b_a * conv) * _silu(z_a)).astype(BF16)
            ca_ref[0, :, cols] = abuf[pad + lt - (CONV_A_W - 1):pad + lt, cols]
            abuf[0:pad, cols] = abuf[lt:lt + pad, cols]
        return run

    def side_out_a(blk):
        def run():
            cols = slice(blk * SIDE_COLS, (blk + 1) * SIDE_COLS)
            ya_scr[:, cols] = jnp.dot(yap_scr[...], woa_ref[:, cols], preferred_element_type=F32)
        return run

    side_work += [side_proj(pa_scr, 0, wmain_ref, 0, blk) for blk in range(4 * d // SIDE_COLS)]
    side_work += [side_proj(pr_scr, 0, wmain_ref, 4 * d + dqkv, blk) for blk in range(d // SIDE_COLS)]
    side_work += [side_proj(pr_scr, d, wgate_ref, 0, blk) for blk in range(2 * d // SIDE_COLS)]
    side_work += [side_branch_a(blk) for blk in range(d // SIDE_COLS)]
    side_work += [side_out_a(blk) for blk in range(d // SIDE_COLS)]
    side_work.reverse()

    def tick():
        if side_work:
            side_work.pop()()

    def conv_qkv(blk):
        cols = slice(blk * SIDE_COLS, (blk + 1) * SIDE_COLS)
        cq = _silu(_causal_taps(qkvbuf, cols, cbw_ref, CONV_B_W, lt, jnp.minimum(t, 0)))
        cb_ref[0, :, cols] = qkvbuf[pad + lt - (CONV_B_W - 1):pad + lt, cols]
        qkvbuf[0:pad, cols] = qkvbuf[lt:lt + pad, cols]
        dst = slice((blk * SIDE_COLS) % d, (blk * SIDE_COLS) % d + SIDE_COLS)
        if blk * SIDE_COLS < d:
            q_scr[:, dst] = _head_l2norm(cq, HEAD ** -0.5)
        elif blk * SIDE_COLS < 2 * d:
            k_scr[:, dst] = _head_l2norm(cq, 1.0)
        else:
            v_scr[:, dst] = cq

    def proj_qkv(blk):
        cols = slice(blk * SIDE_COLS, (blk + 1) * SIDE_COLS)
        qkvbuf[pad:pad + lt, cols] = jnp.dot(hb, wmain_ref[:, 4 * d + blk * SIDE_COLS:4 * d + (blk + 1) * SIDE_COLS],
                                             preferred_element_type=F32)

    proj_qkv(0)
    for blk in range(dqkv // SIDE_COLS):
        if blk + 1 < dqkv // SIDE_COLS:
            proj_qkv(blk + 1)
        else:
            tick()
        conv_qkv(blk)

    ps = jnp.dot(hb, wsmall_ref[...], preferred_element_type=F32)
    pst = lax.dot_general(wsmallt_ref[...], hb, (((1,), (1,)), ((), ())), preferred_element_type=F32)
    b_scr[...] = _sigmoid(ps)
    g_col = -jnp.exp(alc_ref[...]) * _softplus(ps + dtc_ref[...])
    g_row = -jnp.exp(alr_ref[...]) * _softplus(pst + dtr_ref[...])
    row_in_chunk = lax.broadcasted_iota(jnp.int32, (lt, V7X_LANES), 0) % CHUNK
    gc_col = g_col
    shift = 1
    while shift < CHUNK:
        gc_col = gc_col + jnp.where(row_in_chunk >= shift, pltpu.roll(gc_col, shift, axis=0), 0.0)
        shift *= 2
    gc_scr[...] = gc_col
    gt_scr[...] = jnp.concatenate(
        [jnp.broadcast_to(gc_col[(c + 1) * CHUNK - 1:(c + 1) * CHUNK, :], (CHUNK, V7X_LANES))
         for c in range(n_chunks)], axis=0)
    rj = lax.broadcasted_iota(jnp.int32, (CHUNK, CHUNK), 0)
    cj = lax.broadcasted_iota(jnp.int32, (CHUNK, CHUNK), 1)
    triu = jnp.where(rj <= cj, 1.0, 0.0).astype(F32)
    for c in range(n_chunks):
        gcr_scr[c] = _dot_exact(g_row[:, c * CHUNK:(c + 1) * CHUNK], triu)
    tick()

    causal = rj >= cj
    strict = rj > cj
    eye = jnp.where(rj == cj, 1.0, 0.0).astype(F32)

    nb = n_chunks * N_HEADS
    lane = lax.broadcasted_iota(jnp.int32, (CHUNK, 2 * CHUNK), 1)
    low_half = lane < CHUNK
    eye_left = jnp.where(lane == lax.broadcasted_iota(jnp.int32, (CHUNK, 2 * CHUNK), 0), 1.0, 0.0).astype(BF16)

    for c in range(n_chunks):
        rows = slice(c * CHUNK, (c + 1) * CHUNK)
        beta_blk = b_scr[rows, :]
        gc_blk = gc_scr[rows, :]
        gt_blk = gt_scr[rows, :]
        gcr_blk = gcr_scr[c]
        for hd in range(N_HEADS):
            n = c * N_HEADS + hd
            sl = slice(hd * HEAD, (hd + 1) * HEAD)
            qh = q_scr[rows, sl]
            kh = k_scr[rows, sl]
            vh = v_scr[rows, sl]
            bcol = _lane_bcast(beta_blk, hd, HEAD)
            gcol = _lane_bcast(gc_blk, N_HEADS + hd, HEAD)
            gtot = _lane_bcast(gt_blk, N_HEADS + hd, HEAD)
            grow = gcr_blk[N_HEADS + hd:N_HEADS + hd + 1, :]
            decay = jnp.exp(jnp.where(causal, gcol[:, 0:CHUNK] - grow, -jnp.inf))
            egc = jnp.exp(gcol)
            kb = kh * bcol
            kq = _dot_nt(jnp.concatenate([kb, qh], axis=0), kh)
            xm = -jnp.where(strict, kq[0:CHUNK] * decay, 0.0)
            r_scr[n] = jnp.concatenate([xm, eye], axis=-1)
            aqk_scr[n] = (kq[CHUNK:2 * CHUNK] * decay).astype(BF16)
            rhs_scr[n] = jnp.concatenate([vh * bcol, kb * egc], axis=-1)
            qd_scr[n] = qh * egc
            kd_scr[n] = (kh * jnp.exp(gtot - gcol)).astype(BF16)
            egt_scr[n] = jnp.exp(gtot[0:V7X_SUBLANES, :])
            if n % SIDE_EVERY_PREP == SIDE_EVERY_PREP - 1:
                tick()

    def split_parts(r):
        hi = r.astype(BF16)
        hi32 = hi.astype(F32)
        lo32 = r - hi32
        return hi, hi32, lo32

    m = 1
    while m < CHUNK:
        for n in range(nb):
            r = r_scr[n]
            hi, hi32, lo32 = split_parts(r)
            lhs = jnp.where(low_half, hi32, pltpu.roll(lo32, CHUNK, axis=1)).astype(BF16)
            lo = lo32.astype(BF16)
            him = hi - eye_left
            prod = jnp.dot(jnp.concatenate([lhs, lhs], axis=1), jnp.concatenate([him, him, lo, lo], axis=0),
                           preferred_element_type=F32)
            r_scr[n] = prod + r
            if n % SIDE_EVERY_SOLVE == SIDE_EVERY_SOLVE - 1:
                tick()
        m *= 2
    for n in range(nb):
        hi, hi32, lo32 = split_parts(r_scr[n])
        lhs = jnp.where(low_half, pltpu.roll(lo32, CHUNK, axis=1), hi32).astype(BF16)
        bh, _, bl32 = split_parts(rhs_scr[n])
        bl = bl32.astype(BF16)
        sol = jnp.dot(jnp.concatenate([lhs, lhs], axis=1), jnp.concatenate([bh, bh, bl, bl], axis=0),
                      preferred_element_type=F32)
        wu_scr[n] = jnp.concatenate([sol[:, HEAD:2 * HEAD], sol[:, 0:HEAD]], axis=1).astype(BF16)
        if n % SIDE_EVERY == SIDE_EVERY - 1:
            tick()
    for n in range(nb):
        wu = wu_scr[n]
        kw = lax.dot_general(kd_scr[n], wu, (((0,), (0,)), ((), ())), preferred_element_type=F32)
        aw = jnp.dot(aqk_scr[n], wu, preferred_element_type=F32)
        mp_scr[n, 0:HEAD, :] = kw[:, 0:HEAD].astype(BF16)
        mp_scr[n, HEAD:HEAD + CHUNK, :] = (qd_scr[n] - aw[:, 0:HEAD]).astype(BF16)
        sb_scr[n] = kw[:, HEAD:2 * HEAD]
        ob_scr[n] = aw[:, HEAD:2 * HEAD]
        if n % SIDE_EVERY == SIDE_EVERY - 1:
            tick()

    zero_blk = jnp.zeros((HEAD, HEAD), BF16)
    for c in range(n_chunks):
        rows = slice(c * CHUNK, (c + 1) * CHUNK)
        for hd in range(0, N_HEADS, 2):
            n = c * N_HEADS + hd
            s_pair = [s_scr[hd], s_scr[hd + 1]]
            state_diag = jnp.concatenate(
                [jnp.concatenate([s_pair[0].astype(BF16), zero_blk], axis=1),
                 jnp.concatenate([zero_blk, s_pair[1].astype(BF16)], axis=1)], axis=0)
            res2 = jnp.dot(jnp.concatenate([mp_scr[n], mp_scr[n + 1]], axis=1), state_diag,
                           preferred_element_type=F32)
            for i in range(2):
                res = res2[:, i * HEAD:(i + 1) * HEAD]
                sl = slice((hd + i) * HEAD, (hd + i + 1) * HEAD)
                o_scr[rows, sl] = res[HEAD:HEAD + CHUNK] + ob_scr[n + i]
                s_scr[hd + i] = (s_pair[i] * jnp.concatenate([egt_scr[n + i]] * (HEAD // V7X_SUBLANES), axis=0)
                                 - res[0:HEAD] + sb_scr[n + i])
        tick()
    while side_work:
        tick()

    on = _head_rms_gate(o_scr[...], onw_ref[...], pr_scr[:, 0:d])
    yb = jnp.dot(on.astype(BF16), wob_ref[...], preferred_element_type=F32)
    merged = _sigmoid(pr_scr[:, d:2 * d]) * ya_scr[...] + _sigmoid(pr_scr[:, 2 * d:3 * d]) * yb
    post = _rms(jnp.dot(merged.astype(BF16), wo_ref[...], preferred_element_type=F32), npost_ref[...])
    y_ref[0] = x + gate_ref[0] * post

    @pl.when(t == pl.num_programs(1) - 1)
    def _():
        s_ref[0] = s_scr[...]


def _prompt(x, mod, consts):
    bsz, seq, d = x.shape
    dqkv = 3 * d
    lt = PROMPT_TILE
    pad = V7X_SUBLANES
    nb = (lt // CHUNK) * N_HEADS
    vm = pl.BlockSpec(memory_space=pltpu.VMEM)
    shift_spec, scale_spec, gate_spec = [pl.BlockSpec((1, 1, d), lambda b, t, j=j: (b, 0, j)) for j in range(3)]
    return pl.pallas_call(
        _prompt_kernel,
        grid=(bsz, seq // lt),
        in_specs=[pl.BlockSpec((1, lt, d), lambda b, t: (b, t, 0)), shift_spec, scale_spec, gate_spec]
        + [vm] * len(consts),
        out_specs=[pl.BlockSpec((1, lt, d), lambda b, t: (b, t, 0)),
                   pl.BlockSpec((1, CONV_A_W - 1, d), lambda b, t: (b, 0, 0)),
                   pl.BlockSpec((1, CONV_B_W - 1, dqkv), lambda b, t: (b, 0, 0)),
                   pl.BlockSpec((1, N_HEADS, HEAD, HEAD), lambda b, t: (b, 0, 0, 0))],
        out_shape=[jax.ShapeDtypeStruct((bsz, seq, d), F32),
                   jax.ShapeDtypeStruct((bsz, CONV_A_W - 1, d), F32),
                   jax.ShapeDtypeStruct((bsz, CONV_B_W - 1, dqkv), F32),
                   jax.ShapeDtypeStruct((bsz, N_HEADS, HEAD, HEAD), F32)],
        scratch_shapes=[pltpu.VMEM((pad + lt, d), F32),
                        pltpu.VMEM((pad + lt, dqkv), F32),
                        pltpu.VMEM((N_HEADS, HEAD, HEAD), F32),
                        pltpu.VMEM((lt, d), F32), pltpu.VMEM((lt, d), F32),
                        pltpu.VMEM((lt, d), F32), pltpu.VMEM((lt, d), F32),
                        pltpu.VMEM((lt, V7X_LANES), F32), pltpu.VMEM((lt, V7X_LANES), F32),
                        pltpu.VMEM((lt, V7X_LANES), F32),
                        pltpu.VMEM((lt // CHUNK, 2 * N_HEADS, CHUNK), F32),
                        pltpu.VMEM((nb, CHUNK, 2 * CHUNK), F32),
                        pltpu.VMEM((nb, CHUNK, CHUNK), BF16), pltpu.VMEM((nb, CHUNK, 2 * HEAD), F32),
                        pltpu.VMEM((nb, CHUNK, HEAD), F32), pltpu.VMEM((nb, CHUNK, HEAD), BF16),
                        pltpu.VMEM((nb, V7X_SUBLANES, HEAD), F32), pltpu.VMEM((nb, HEAD + CHUNK, HEAD), BF16),
                        pltpu.VMEM((nb, HEAD, HEAD), F32), pltpu.VMEM((nb, CHUNK, HEAD), F32),
                        pltpu.VMEM((nb, CHUNK, 2 * HEAD), BF16),
                        pltpu.VMEM((lt, 4 * d), F32), pltpu.VMEM((lt, 3 * d), F32),
                        pltpu.VMEM((lt, d), BF16), pltpu.VMEM((lt, d), F32)],
        compiler_params=pltpu.CompilerParams(dimension_semantics=("arbitrary", "arbitrary"),
                                             vmem_limit_bytes=V7X_VMEM_LIMIT_BYTES),
        name="prompt",
    )(x, mod, mod, mod, *consts)


def _sample_front_kernel(row0, x_ref, mod_ref, sa_ref, sb_ref, npre_ref, wmain_ref, wgate_ref, wsmall_ref,
                         caw_ref, cbw_ref, alc_ref, dtc_ref, woa_ref,
                         q_ref, k_ref, v_ref, be_ref, eg_ref, ya_ref, rest_ref, nsa_ref, nsb_ref):
    d = x_ref.shape[1]
    dqkv = 3 * d
    n = x_ref.shape[0]
    x = x_ref[...]
    h = _rms(x, npre_ref[...]) * (1.0 + mod_ref[row0:row0 + n, d:2 * d]) + mod_ref[row0:row0 + n, 0:d]
    hb = h.astype(BF16)

    pa = jnp.dot(hb, wmain_ref[:, 0:4 * d], preferred_element_type=F32)
    u = pa[:, d:2 * d] * pa[:, 0:d]
    conv = u * caw_ref[CONV_A_W - 1:CONV_A_W, :]
    for j in range(CONV_A_W - 1):
        conv = conv + sa_ref[j] * caw_ref[j:j + 1, :]
    ya_ref[...] = jnp.dot(((pa[:, 2 * d:3 * d] * conv) * _silu(pa[:, 3 * d:4 * d])).astype(BF16), woa_ref[...],
                          preferred_element_type=F32)
    for j in range(CONV_A_W - 2):
        nsa_ref[j] = sa_ref[j + 1]
    nsa_ref[CONV_A_W - 2] = u

    pq = jnp.dot(hb, wmain_ref[:, 4 * d:4 * d + dqkv], preferred_element_type=F32)
    cq = pq * cbw_ref[CONV_B_W - 1:CONV_B_W, :]
    for j in range(CONV_B_W - 1):
        cq = cq + sb_ref[j] * cbw_ref[j:j + 1, :]
    cq = _silu(cq)
    for j in range(CONV_B_W - 2):
        nsb_ref[j] = sb_ref[j + 1]
    nsb_ref[CONV_B_W - 2] = pq
    q_ref[...] = _head_l2norm(cq[:, 0:d], HEAD ** -0.5)
    k_ref[...] = _head_l2norm(cq[:, d:2 * d], 1.0)
    v_ref[...] = cq[:, 2 * d:3 * d]

    rest_ref[:, 0:d] = jnp.dot(hb, wmain_ref[:, 4 * d + dqkv:5 * d + dqkv], preferred_element_type=F32)
    rest_ref[:, d:3 * d] = jnp.dot(hb, wgate_ref[...], preferred_element_type=F32)

    ps = jnp.dot(hb, wsmall_ref[...], preferred_element_type=F32)
    beta = _sigmoid(ps)
    eg = jnp.exp(-jnp.exp(alc_ref[...]) * _softplus(ps + dtc_ref[...]))
    be_ref[...] = jnp.concatenate([_lane_bcast(beta, hd, HEAD) for hd in range(N_HEADS)], axis=-1)
    eg_ref[...] = jnp.concatenate([_lane_bcast(eg, N_HEADS + hd, HEAD) for hd in range(N_HEADS)], axis=-1)


def _sample_state_kernel(s_ref, q_ref, k_ref, v_ref, be_ref, eg_ref, so_ref, o_ref):
    bb = q_ref.shape[0]
    row = lax.broadcasted_iota(jnp.int32, (bb, HEAD), 0)
    erow = lax.broadcasted_iota(jnp.int32, (bb, bb * HEAD), 0)
    elane = lax.broadcasted_iota(jnp.int32, (bb, bb * HEAD), 1)
    diag = (elane // HEAD) == erow
    zeros_pad = jnp.zeros((2 * V7X_SUBLANES - bb, HEAD), F32)
    heads = [slice(hd * HEAD, (hd + 1) * HEAD) for hd in range(N_HEADS)]
    ks_all, qs_all = [], []
    for sl in heads:
        kq = jnp.concatenate([k_ref[:, sl], q_ref[:, sl]], axis=0).astype(BF16)
        hd = sl.start // HEAD
        prods = [jnp.dot(kq, s_ref[tk, hd].astype(BF16), preferred_element_type=F32) for tk in range(bb)]
        ks = prods[0][0:bb]
        qs = prods[0][bb:2 * bb]
        for tk in range(1, bb):
            ks = jnp.where(row == tk, prods[tk][0:bb], ks)
            qs = jnp.where(row == tk, prods[tk][bb:2 * bb], qs)
        ks_all.append(ks)
        qs_all.append(qs)
    outers = []
    for sl, ks, qs in zip(heads, ks_all, qs_all):
        qh = q_ref[:, sl]
        kh = k_ref[:, sl]
        eg = eg_ref[:, sl]
        vn = be_ref[:, sl] * (v_ref[:, sl] - eg * ks)
        o_ref[:, sl] = eg * qs + jnp.sum(qh * kh, axis=-1, keepdims=True) * vn
        vexp = jnp.where(diag, jnp.concatenate([vn] * bb, axis=-1), 0.0)
        outers.append(_dot_tn(jnp.concatenate([kh, zeros_pad], axis=0),
                              jnp.concatenate([vexp, jnp.zeros((2 * V7X_SUBLANES - bb, bb * HEAD), F32)], axis=0)))
    for sl, outer in zip(heads, outers):
        hd = sl.start // HEAD
        eg = eg_ref[:, sl]
        for tk in range(bb):
            so_ref[tk, hd] = (s_ref[tk, hd] * jnp.broadcast_to(eg[tk:tk + 1, :], (HEAD, HEAD))
                              + outer[:, tk * HEAD:(tk + 1) * HEAD])


def _sample_back_kernel(row0, x_ref, mod_ref, o_ref, ya_ref, rest_ref, onw_ref, wob_ref, wo_ref, npost_ref, y_ref):
    n, d = x_ref.shape
    on = _head_rms_gate(o_ref[...], onw_ref[...], rest_ref[:, 0:d])
    yb = jnp.dot(on.astype(BF16), wob_ref[...], preferred_element_type=F32)
    merged = _sigmoid(rest_ref[:, d:2 * d]) * ya_ref[...] + _sigmoid(rest_ref[:, 2 * d:3 * d]) * yb
    post = _rms(jnp.dot(merged.astype(BF16), wo_ref[...], preferred_element_type=F32), npost_ref[...])
    y_ref[...] = x_ref[...] + mod_ref[row0:row0 + n, 2 * d:3 * d] * post


def _sample(x, mod, row0, sa, sb, s0, w):
    n, d = x.shape
    dqkv = 3 * d
    vm = pl.BlockSpec(memory_space=pltpu.VMEM)
    params = pltpu.CompilerParams(vmem_limit_bytes=V7X_VMEM_LIMIT_BYTES)
    row = jax.ShapeDtypeStruct((n, d), F32)
    front_in = (x, mod, sa, sb, w["npre"], w["wmain"], w["wgate"], w["wsmall"], w["caw"], w["cbw"],
                w["alc"], w["dtc"], w["woa"])
    q, k, v, be, eg, ya, rest, nsa, nsb = pl.pallas_call(
        functools.partial(_sample_front_kernel, row0),
        in_specs=[vm] * len(front_in),
        out_specs=[vm] * 9,
        out_shape=[row, row, row, row, row, row, jax.ShapeDtypeStruct((n, 3 * d), F32),
                   jax.ShapeDtypeStruct(sa.shape, F32), jax.ShapeDtypeStruct(sb.shape, F32)],
        compiler_params=params,
        name="sample_front",
    )(*front_in)

    bb = SAMPLE_BLOCK
    tok = pl.BlockSpec((bb, d), lambda i: (i, 0))
    st = pl.BlockSpec((bb, N_HEADS, HEAD, HEAD), lambda i: (i, 0, 0, 0))
    s_new, o = pl.pallas_call(
        _sample_state_kernel,
        grid=(n // bb,),
        in_specs=[st, tok, tok, tok, tok, tok],
        out_specs=[st, tok],
        out_shape=[jax.ShapeDtypeStruct(s0.shape, F32), row],
        compiler_params=pltpu.CompilerParams(dimension_semantics=("arbitrary",),
                                             vmem_limit_bytes=V7X_VMEM_LIMIT_BYTES),
        name="sample_state",
    )(s0, q, k, v, be, eg)

    back_in = (x, mod, o, ya, rest, w["onw"], w["wob"], w["wo"], w["npost"])
    y = pl.pallas_call(
        functools.partial(_sample_back_kernel, row0),
        in_specs=[vm] * len(back_in),
        out_specs=vm,
        out_shape=row,
        compiler_params=params,
        name="sample_back",
    )(*back_in)
    return y, nsa, nsb, s_new


def kernel(x_prompt, x_sample, c_prompt, c_sample, state_conv_a, state_conv_qkv, state_delta, ada_w, ada_b,
           norm_pre, w_in, conv_a_w, conv_b_w, a_log, dt_bias, onorm_w, w_out_a, w_out_b, w_o, norm_post):
    depth = w_in.shape[0]
    assert depth == 1, "single-layer trunk"
    bp, seq, d = x_prompt.shape
    ns = x_sample.shape[0]
    dqkv = 3 * d
    assert d == N_HEADS * HEAD and seq % PROMPT_TILE == 0 and PROMPT_TILE % CHUNK == 0 and 2 * CHUNK == V7X_LANES
    assert ns % SAMPLE_BLOCK == 0 and x_sample.shape[1] == 1

    off_small = 4 * d + dqkv + d
    w0 = w_in.reshape(w_in.shape[1:])
    wmain = w0.astype(BF16)
    wgate = wmain[:, off_small + 2 * N_HEADS:]
    wsmall = jnp.pad(wmain[:, off_small:off_small + 2 * N_HEADS], ((0, 0), (0, V7X_LANES - 2 * N_HEADS)))
    wsmallt = wmain[:, off_small:off_small + 2 * N_HEADS].T
    zeros_h = jnp.zeros((N_HEADS,), F32)
    lane_pad = jnp.zeros((V7X_LANES - 2 * N_HEADS,), F32)
    w = {
        "npre": norm_pre, "npost": norm_post, "onw": onorm_w,
        "wmain": wmain, "wgate": wgate, "wsmall": wsmall, "wsmallt": wsmallt,
        "caw": conv_a_w[0], "cbw": conv_b_w[0],
        "alc": jnp.concatenate([zeros_h, a_log[0], lane_pad])[None, :],
        "dtc": jnp.concatenate([zeros_h, dt_bias[0], lane_pad])[None, :],
        "alr": jnp.concatenate([zeros_h, a_log[0]])[:, None],
        "dtr": jnp.concatenate([zeros_h, dt_bias[0]])[:, None],
        "woa": w_out_a[0].astype(BF16), "wob": w_out_b[0].astype(BF16), "wo": w_o[0].astype(BF16),
    }

    mod = _adaln(jnp.concatenate([c_prompt, c_sample], axis=0), ada_w.reshape(ada_w.shape[1:]), ada_b)
    consts = (w["npre"], w["wmain"], w["wgate"], w["wsmall"], w["wsmallt"], w["caw"], w["cbw"], w["alc"], w["dtc"],
              w["alr"], w["dtr"], w["onw"], w["woa"], w["wob"], w["wo"], w["npost"])
    yp, pca, pcb, pds = _prompt(x_prompt, mod.reshape(mod.shape[0], 1, 3 * d), consts)

    ys, nsa, nsb, sds = _sample(
        x_sample.reshape(ns, d), mod, bp,
        jnp.transpose(state_conv_a.reshape(ns, CONV_A_W - 1, d), (1, 0, 2)),
        jnp.transpose(state_conv_qkv.reshape(ns, CONV_B_W - 1, dqkv), (1, 0, 2)),
        state_delta.reshape(state_delta.shape[1:]), w)

    return (yp, ys.reshape(ns, 1, d), pca.reshape((1,) + pca.shape), pcb.reshape((1,) + pcb.shape),
            pds.reshape((1,) + pds.shape).astype(state_delta.dtype),
            jnp.transpose(nsa, (1, 0, 2)).reshape(1, ns, CONV_A_W - 1, d),
            jnp.transpose(nsb, (1, 0, 2)).reshape(1, ns, CONV_B_W - 1, dqkv),
            sds.reshape((1,) + sds.shape).astype(state_delta.dtype))
```

```python
import functools

import jax
import jax.numpy as jnp
from jax import lax
from jax.experimental import pallas as pl
from jax.experimental.pallas import tpu as pltpu

F32 = jnp.float32
BF16 = jnp.bfloat16

EPS = 1e-6
N_HEADS = 8
HEAD = 128
CONV_A_W = 3
CONV_B_W = 4

V7X_LANES = 128
V7X_SUBLANES = 8
V7X_VMEM_LIMIT_BYTES = 60 * 1024 * 1024

PROMPT_TILE = 256
SIDE_COLS = 512
SIDE_EVERY_PREP = 4
SIDE_EVERY_SOLVE = 32
SIDE_EVERY = 16
CHUNK = 64
SAMPLE_BLOCK = 8

HI = lax.Precision.HIGHEST


NEG_LOG2_E = -1.4426950408889634


def _sigmoid(x):
    return 1.0 / (1.0 + jnp.exp2(x * NEG_LOG2_E))


def _silu(x):
    return x * _sigmoid(x)


def _softplus(x):
    return jnp.maximum(x, 0.0) + jnp.log1p(jnp.exp(-jnp.abs(x)))


def _dot(a, b):
    return jnp.dot(a.astype(BF16), b.astype(BF16), preferred_element_type=F32)


def _dot_nt(a, b):
    return lax.dot_general(a.astype(BF16), b.astype(BF16), (((1,), (1,)), ((), ())),
                           preferred_element_type=F32)


def _dot_tn(a, b):
    return lax.dot_general(a.astype(BF16), b.astype(BF16), (((0,), (0,)), ((), ())),
                           preferred_element_type=F32)


def _dot_exact(a, b):
    return jnp.dot(a, b, precision=HI, preferred_element_type=F32)


def _rms(x, w):
    return x * lax.rsqrt(jnp.mean(x * x, axis=-1, keepdims=True) + EPS) * w


def _head_l2norm(x, scale):
    outs = []
    for h in range(x.shape[1] // HEAD):
        xh = x[:, h * HEAD:(h + 1) * HEAD]
        outs.append(xh * (lax.rsqrt(jnp.sum(xh * xh, axis=-1, keepdims=True) + EPS) * scale))
    return jnp.concatenate(outs, axis=-1)


def _head_rms_gate(o, w, z):
    outs = []
    for h in range(N_HEADS):
        sl = slice(h * HEAD, (h + 1) * HEAD)
        oh = o[:, sl]
        outs.append(oh * lax.rsqrt(jnp.mean(oh * oh, axis=-1, keepdims=True) + EPS) * w * _silu(z[:, sl]))
    return jnp.concatenate(outs, axis=-1)


def _causal_taps(buf_ref, cols, w_ref, width, rows, zero):
    sub = V7X_SUBLANES
    n_tiles = rows // sub
    tiles = [buf_ref[i * sub:(i + 1) * sub, cols] for i in range(n_tiles + 1)]
    width_cols = tiles[0].shape[1]
    row = lax.broadcasted_iota(jnp.int32, (sub, width_cols), 0)
    w_last = jnp.broadcast_to(w_ref[width - 1:width, cols], (sub, width_cols))
    acc = [tiles[i + 1] * w_last for i in range(n_tiles)]
    for j in range(width - 1):
        s = width - 1 - j
        w_j = jnp.broadcast_to(w_ref[j:j + 1, cols], (sub, width_cols))
        rots = [pltpu.roll(tl, s + zero, axis=0) for tl in tiles]
        from_above = row < s
        acc = [acc[i] + jnp.where(from_above, rots[i], rots[i + 1]) * w_j for i in range(n_tiles)]
    return jnp.concatenate(acc, axis=0)


def _lane_bcast(x, col, width):
    return jnp.broadcast_to(x[:, col:col + 1], (x.shape[0], width))


def _adaln_kernel(c_ref, w_ref, b_ref, o_ref):
    o_ref[...] = _dot(_silu(c_ref[...]), w_ref[...]) + b_ref[...]


def _adaln(c_all, ada_w, ada_b):
    n, d = c_all.shape
    cols = ada_w.shape[1]
    return pl.pallas_call(
        _adaln_kernel,
        grid=(cols // d,),
        in_specs=[pl.BlockSpec((n, d), lambda j: (0, 0)),
                  pl.BlockSpec((d, d), lambda j: (0, j)),
                  pl.BlockSpec((1, d), lambda j: (0, j))],
        out_specs=pl.BlockSpec((n, d), lambda j: (0, j)),
        out_shape=jax.ShapeDtypeStruct((n, cols), F32),
        name="adaln",
    )(c_all, ada_w, ada_b)


def _prompt_kernel(x_ref, shift_ref, scale_ref, gate_ref, npre_ref, wmain_ref, wgate_ref, wsmall_ref, wsmallt_ref,
                   caw_ref, cbw_ref, alc_ref, dtc_ref, alr_ref, dtr_ref, onw_ref, woa_ref, wob_ref, wo_ref,
                   npost_ref,
                   y_ref, ca_ref, cb_ref, s_ref,
                   abuf, qkvbuf, s_scr, q_scr, k_scr, v_scr, o_scr, b_scr, gc_scr, gt_scr, gcr_scr,
                   r_scr, aqk_scr, rhs_scr, qd_scr, kd_scr, egt_scr, mp_scr, sb_scr, ob_scr, wu_scr,
                   pa_scr, pr_scr, yap_scr, ya_scr):
    lt = x_ref.shape[1]
    d = x_ref.shape[2]
    dqkv = qkvbuf.shape[1]
    n_chunks = lt // CHUNK
    pad = V7X_SUBLANES
    t = pl.program_id(1)

    @pl.when(t == 0)
    def _():
        abuf[0:pad, :] = jnp.zeros((pad, d), F32)
        qkvbuf[0:pad, :] = jnp.zeros((pad, dqkv), F32)
        s_scr[...] = jnp.zeros(s_scr.shape, F32)

    x = x_ref[0]
    h = _rms(x, npre_ref[...]) * (1.0 + scale_ref[0]) + shift_ref[0]
    hb = h.astype(BF16)

    side_work = []

    def side_proj(dst_ref, dst_off, w_ref, w_off, blk):
        def run():
            dst_ref[:, dst_off + blk * SIDE_COLS:dst_off + (blk + 1) * SIDE_COLS] = jnp.dot(
                hb, w_ref[:, w_off + blk * SIDE_COLS:w_off + (blk + 1) * SIDE_COLS], preferred_element_type=F32)
        return run

    def side_branch_a(blk):
        def run():
            cols = slice(blk * SIDE_COLS, (blk + 1) * SIDE_COLS)
            u = pa_scr[:, d + blk * SIDE_COLS:d + (blk + 1) * SIDE_COLS] * pa_scr[:, cols]
            abuf[pad:pad + lt, cols] = u
            conv = _causal_taps(abuf, cols, caw_ref, CONV_A_W, lt, jnp.minimum(t, 0))
            b_a = pa_scr[:, 2 * d + blk * SIDE_COLS:2 * d + (blk + 1) * SIDE_COLS]
            z_a = pa_scr[:, 3 * d + blk * SIDE_COLS:3 * d + (blk + 1) * SIDE_COLS]
            yap_scr[:, cols] = ((b_a * conv) * _silu(z_a)).astype(BF16)
            ca_ref[0, :, cols] = abuf[pad + lt - (CONV_A_W - 1):pad + lt, cols]
            abuf[0:pad, cols] = abuf[lt:lt + pad, cols]
        return run

    def side_out_a(blk):
        def run():
            cols = slice(blk * SIDE_COLS, (blk + 1) * SIDE_COLS)
            ya_scr[:, cols] = jnp.dot(yap_scr[...], woa_ref[:, cols], preferred_element_type=F32)
        return run

    side_work += [side_proj(pa_scr, 0, wmain_ref, 0, blk) for blk in range(4 * d // SIDE_COLS)]
    side_work += [side_proj(pr_scr, 0, wmain_ref, 4 * d + dqkv, blk) for blk in range(d // SIDE_COLS)]
    side_work += [side_proj(pr_scr, d, wgate_ref, 0, blk) for blk in range(2 * d // SIDE_COLS)]
    side_work += [side_branch_a(blk) for blk in range(d // SIDE_COLS)]
    side_work += [side_out_a(blk) for blk in range(d // SIDE_COLS)]
    side_work.reverse()

    def tick():
        if side_work:
            side_work.pop()()

    def conv_qkv(blk):
        cols = slice(blk * SIDE_COLS, (blk + 1) * SIDE_COLS)
        cq = _silu(_causal_taps(qkvbuf, cols, cbw_ref, CONV_B_W, lt, jnp.minimum(t, 0)))
        cb_ref[0, :, cols] = qkvbuf[pad + lt - (CONV_B_W - 1):pad + lt, cols]
        qkvbuf[0:pad, cols] = qkvbuf[lt:lt + pad, cols]
        dst = slice((blk * SIDE_COLS) % d, (blk * SIDE_COLS) % d + SIDE_COLS)
        if blk * SIDE_COLS < d:
            q_scr[:, dst] = _head_l2norm(cq, HEAD ** -0.5)
        elif blk * SIDE_COLS < 2 * d:
            k_scr[:, dst] = _head_l2norm(cq, 1.0)
        else:
            v_scr[:, dst] = cq

    def proj_qkv(blk):
        cols = slice(blk * SIDE_COLS, (blk + 1) * SIDE_COLS)
        qkvbuf[pad:pad + lt, cols] = jnp.dot(hb, wmain_ref[:, 4 * d + blk * SIDE_COLS:4 * d + (blk + 1) * SIDE_COLS],
                                             preferred_element_type=F32)

    proj_qkv(0)
    for blk in range(dqkv // SIDE_COLS):
        if blk + 1 < dqkv // SIDE_COLS:
            proj_qkv(blk + 1)
        else:
            tick()
        conv_qkv(blk)

    ps = jnp.dot(hb, wsmall_ref[...], preferred_element_type=F32)
    pst = lax.dot_general(wsmallt_ref[...], hb, (((1,), (1,)), ((), ())), preferred_element_type=F32)
    b_scr[...] = _sigmoid(ps)
    g_col = -jnp.exp(alc_ref[...]) * _softplus(ps + dtc_ref[...])
    g_row = -jnp.exp(alr_ref[...]) * _softplus(pst + dtr_ref[...])
    row_in_chunk = lax.broadcasted_iota(jnp.int32, (lt, V7X_LANES), 0) % CHUNK
    gc_col = g_col
    shift = 1
    while shift < CHUNK:
        gc_col = gc_col + jnp.where(row_in_chunk >= shift, pltpu.roll(gc_col, shift, axis=0), 0.0)
        shift *= 2
    gc_scr[...] = gc_col
    gt_scr[...] = jnp.concatenate(
        [jnp.broadcast_to(gc_col[(c + 1) * CHUNK - 1:(c + 1) * CHUNK, :], (CHUNK, V7X_LANES))
         for c in range(n_chunks)], axis=0)
    rj = lax.broadcasted_iota(jnp.int32, (CHUNK, CHUNK), 0)
    cj = lax.broadcasted_iota(jnp.int32, (CHUNK, CHUNK), 1)
    triu = jnp.where(rj <= cj, 1.0, 0.0).astype(F32)
    for c in range(n_chunks):
        gcr_scr[c] = _dot_exact(g_row[:, c * CHUNK:(c + 1) * CHUNK], triu)
    tick()

    causal = rj >= cj
    strict = rj > cj
    eye = jnp.where(rj == cj, 1.0, 0.0).astype(F32)

    nb = n_chunks * N_HEADS
    lane = lax.broadcasted_iota(jnp.int32, (CHUNK, 2 * CHUNK), 1)
    low_half = lane < CHUNK
    eye_left = jnp.where(lane == lax.broadcasted_iota(jnp.int32, (CHUNK, 2 * CHUNK), 0), 1.0, 0.0).astype(BF16)

    for c in range(n_chunks):
        rows = slice(c * CHUNK, (c + 1) * CHUNK)
        beta_blk = b_scr[rows, :]
        gc_blk = gc_scr[rows, :]
        gt_blk = gt_scr[rows, :]
        gcr_blk = gcr_scr[c]
        for hd in range(N_HEADS):
            n = c * N_HEADS + hd
            sl = slice(hd * HEAD, (hd + 1) * HEAD)
            qh = q_scr[rows, sl]
            kh = k_scr[rows, sl]
            vh = v_scr[rows, sl]
            bcol = _lane_bcast(beta_blk, hd, HEAD)
            gcol = _lane_bcast(gc_blk, N_HEADS + hd, HEAD)
            gtot = _lane_bcast(gt_blk, N_HEADS + hd, HEAD)
            grow = gcr_blk[N_HEADS + hd:N_HEADS + hd + 1, :]
            decay = jnp.exp(jnp.where(causal, gcol[:, 0:CHUNK] - grow, -jnp.inf))
            egc = jnp.exp(gcol)
            kb = kh * bcol
            kq = _dot_nt(jnp.concatenate([kb, qh], axis=0), kh)
            xm = -jnp.where(strict, kq[0:CHUNK] * decay, 0.0)
            r_scr[n] = jnp.concatenate([xm, eye], axis=-1)
            aqk_scr[n] = (kq[CHUNK:2 * CHUNK] * decay).astype(BF16)
            rhs_scr[n] = jnp.concatenate([vh * bcol, kb * egc], axis=-1)
            qd_scr[n] = qh * egc
            kd_scr[n] = (kh * jnp.exp(gtot - gcol)).astype(BF16)
            egt_scr[n] = jnp.exp(gtot[0:V7X_SUBLANES, :])
            if n % SIDE_EVERY_PREP == SIDE_EVERY_PREP - 1:
                tick()

    def split_parts(r):
        hi = r.astype(BF16)
        hi32 = hi.astype(F32)
        lo32 = r - hi32
        return hi, hi32, lo32

    m = 1
    while m < CHUNK:
        for n in range(nb):
            r = r_scr[n]
            hi, hi32, lo32 = split_parts(r)
            lhs = jnp.where(low_half, hi32, pltpu.roll(lo32, CHUNK, axis=1)).astype(BF16)
            lo = lo32.astype(BF16)
            him = hi - eye_left
            prod = jnp.dot(jnp.concatenate([lhs, lhs], axis=1), jnp.concatenate([him, him, lo, lo], axis=0),
                           preferred_element_type=F32)
            r_scr[n] = prod + r
            if n % SIDE_EVERY_SOLVE == SIDE_EVERY_SOLVE - 1:
                tick()
        m *= 2
    for n in range(nb):
        hi, hi32, lo32 = split_parts(r_scr[n])
        lhs = jnp.where(low_half, pltpu.roll(lo32, CHUNK, axis=1), hi32).astype(BF16)
        bh, _, bl32 = split_parts(rhs_scr[n])
        bl = bl32.astype(BF16)
        sol = jnp.dot(jnp.concatenate([lhs, lhs], axis=1), jnp.concatenate([bh, bh, bl, bl], axis=0),
                      preferred_element_type=F32)
        wu_scr[n] = jnp.concatenate([sol[:, HEAD:2 * HEAD], sol[:, 0:HEAD]], axis=1).astype(BF16)
        if n % SIDE_EVERY == SIDE_EVERY - 1:
            tick()
    for n in range(nb):
        wu = wu_scr[n]
        kw = lax.dot_general(kd_scr[n], wu, (((0,), (0,)), ((), ())), preferred_element_type=F32)
        aw = jnp.dot(aqk_scr[n], wu, preferred_element_type=F32)
        mp_scr[n, 0:HEAD, :] = kw[:, 0:HEAD].astype(BF16)
        mp_scr[n, HEAD:HEAD + CHUNK, :] = (qd_scr[n] - aw[:, 0:HEAD]).astype(BF16)
        sb_scr[n] = kw[:, HEAD:2 * HEAD]
        ob_scr[n] = aw[:, HEAD:2 * HEAD]
        if n % SIDE_EVERY == SIDE_EVERY - 1:
            tick()

    zero_blk = jnp.zeros((HEAD, HEAD), BF16)
    for c in range(n_chunks):
        rows = slice(c * CHUNK, (c + 1) * CHUNK)
        for hd in range(0, N_HEADS, 2):
            n = c * N_HEADS + hd
            s_pair = [s_scr[hd], s_scr[hd + 1]]
            state_diag = jnp.concatenate(
                [jnp.concatenate([s_pair[0].astype(BF16), zero_blk], axis=1),
                 jnp.concatenate([zero_blk, s_pair[1].astype(BF16)], axis=1)], axis=0)
            res2 = jnp.dot(jnp.concatenate([mp_scr[n], mp_scr[n + 1]], axis=1), state_diag,
                           preferred_element_type=F32)
            for i in range(2):
                res = res2[:, i * HEAD:(i + 1) * HEAD]
                sl = slice((hd + i) * HEAD, (hd + i + 1) * HEAD)
                o_scr[rows, sl] = res[HEAD:HEAD + CHUNK] + ob_scr[n + i]
                s_scr[hd + i] = (s_pair[i] * jnp.concatenate([egt_scr[n + i]] * (HEAD // V7X_SUBLANES), axis=0)
                                 - res[0:HEAD] + sb_scr[n + i])
        tick()
    while side_work:
        tick()

    on = _head_rms_gate(o_scr[...], onw_ref[...], pr_scr[:, 0:d])
    yb = jnp.dot(on.astype(BF16), wob_ref[...], preferred_element_type=F32)
    merged = _sigmoid(pr_scr[:, d:2 * d]) * ya_scr[...] + _sigmoid(pr_scr[:, 2 * d:3 * d]) * yb
    post = _rms(jnp.dot(merged.astype(BF16), wo_ref[...], preferred_element_type=F32), npost_ref[...])
    y_ref[0] = x + gate_ref[0] * post

    @pl.when(t == pl.num_programs(1) - 1)
    def _():
        s_ref[0] = s_scr[...]


def _prompt(x, mod, consts):
    bsz, seq, d = x.shape
    dqkv = 3 * d
    lt = PROMPT_TILE
    pad = V7X_SUBLANES
    nb = (lt // CHUNK) * N_HEADS
    vm = pl.BlockSpec(memory_space=pltpu.VMEM)
    shift_spec, scale_spec, gate_spec = [pl.BlockSpec((1, 1, d), lambda b, t, j=j: (b, 0, j)) for j in range(3)]
    return pl.pallas_call(
        _prompt_kernel,
        grid=(bsz, seq // lt),
        in_specs=[pl.BlockSpec((1, lt, d), lambda b, t: (b, t, 0)), shift_spec, scale_spec, gate_spec]
        + [vm] * len(consts),
        out_specs=[pl.BlockSpec((1, lt, d), lambda b, t: (b, t, 0)),
                   pl.BlockSpec((1, CONV_A_W - 1, d), lambda b, t: (b, 0, 0)),
                   pl.BlockSpec((1, CONV_B_W - 1, dqkv), lambda b, t: (b, 0, 0)),
                   pl.BlockSpec((1, N_HEADS, HEAD, HEAD), lambda b, t: (b, 0, 0, 0))],
        out_shape=[jax.ShapeDtypeStruct((bsz, seq, d), F32),
                   jax.ShapeDtypeStruct((bsz, CONV_A_W - 1, d), F32),
                   jax.ShapeDtypeStruct((bsz, CONV_B_W - 1, dqkv), F32),
                   jax.ShapeDtypeStruct((bsz, N_HEADS, HEAD, HEAD), F32)],
        scratch_shapes=[pltpu.VMEM((pad + lt, d), F32),
                        pltpu.VMEM((pad + lt, dqkv), F32),
                        pltpu.VMEM((N_HEADS, HEAD, HEAD), F32),
                        pltpu.VMEM((lt, d), F32), pltpu.VMEM((lt, d), F32),
                        pltpu.VMEM((lt, d), F32), pltpu.VMEM((lt, d), F32),
                        pltpu.VMEM((lt, V7X_LANES), F32), pltpu.VMEM((lt, V7X_LANES), F32),
                        pltpu.VMEM((lt, V7X_LANES), F32),
                        pltpu.VMEM((lt // CHUNK, 2 * N_HEADS, CHUNK), F32),
                        pltpu.VMEM((nb, CHUNK, 2 * CHUNK), F32),
                        pltpu.VMEM((nb, CHUNK, CHUNK), BF16), pltpu.VMEM((nb, CHUNK, 2 * HEAD), F32),
                        pltpu.VMEM((nb, CHUNK, HEAD), F32), pltpu.VMEM((nb, CHUNK, HEAD), BF16),
                        pltpu.VMEM((nb, V7X_SUBLANES, HEAD), F32), pltpu.VMEM((nb, HEAD + CHUNK, HEAD), BF16),
                        pltpu.VMEM((nb, HEAD, HEAD), F32), pltpu.VMEM((nb, CHUNK, HEAD), F32),
                        pltpu.VMEM((nb, CHUNK, 2 * HEAD), BF16),
                        pltpu.VMEM((lt, 4 * d), F32), pltpu.VMEM((lt, 3 * d), F32),
                        pltpu.VMEM((lt, d), BF16), pltpu.VMEM((lt, d), F32)],
        compiler_params=pltpu.CompilerParams(dimension_semantics=("arbitrary", "arbitrary"),
                                             vmem_limit_bytes=V7X_VMEM_LIMIT_BYTES),
        name="prompt",
    )(x, mod, mod, mod, *consts)


def _sample_front_kernel(row0, x_ref, mod_ref, sa_ref, sb_ref, npre_ref, wblk_ref, wgate_ref, wsmall_ref,
                         caw_ref, cbw_ref, alc_ref, dtc_ref, woa_ref,
                         q_ref, k_ref, v_ref, be_ref, eg_ref, ya_ref, rest_ref, nsa_ref, nsb_ref,
                         hb_scr, proj_scr):
    n, d = x_ref.shape
    j = pl.program_id(0)

    @pl.when(j == 0)
    def _():
        h = _rms(x_ref[...], npre_ref[...]) * (1.0 + mod_ref[row0:row0 + n, d:2 * d]) + mod_ref[row0:row0 + n, 0:d]
        hb_scr[...] = h.astype(BF16)

    proj_scr[j] = jnp.dot(hb_scr[...], wblk_ref[...], preferred_element_type=F32)

    @pl.when(j == pl.num_programs(0) - 1)
    def _():
        hb = hb_scr[...]
        u = proj_scr[1] * proj_scr[0]
        conv = u * caw_ref[CONV_A_W - 1:CONV_A_W, :]
        for tap in range(CONV_A_W - 1):
            conv = conv + sa_ref[tap] * caw_ref[tap:tap + 1, :]
        ya_ref[...] = jnp.dot(((proj_scr[2] * conv) * _silu(proj_scr[3])).astype(BF16), woa_ref[...],
                              preferred_element_type=F32)
        for tap in range(CONV_A_W - 2):
            nsa_ref[tap] = sa_ref[tap + 1]
        nsa_ref[CONV_A_W - 2] = u

        for part, (dst_ref, scale) in enumerate(((q_ref, HEAD ** -0.5), (k_ref, 1.0), (v_ref, None))):
            cols = slice(part * d, (part + 1) * d)
            pq = proj_scr[4 + part]
            cq = pq * cbw_ref[CONV_B_W - 1:CONV_B_W, cols]
            for tap in range(CONV_B_W - 1):
                cq = cq + sb_ref[tap, :, cols] * cbw_ref[tap:tap + 1, cols]
            cq = _silu(cq)
            nsb_ref[CONV_B_W - 2, :, cols] = pq
            dst_ref[...] = cq if scale is None else _head_l2norm(cq, scale)
        for tap in range(CONV_B_W - 2):
            nsb_ref[tap] = sb_ref[tap + 1]

        rest_ref[:, 0:d] = proj_scr[7]
        rest_ref[:, d:3 * d] = jnp.dot(hb, wgate_ref[...], preferred_element_type=F32)

        ps = jnp.dot(hb, wsmall_ref[...], preferred_element_type=F32)
        beta = _sigmoid(ps)
        eg = jnp.exp(-jnp.exp(alc_ref[...]) * _softplus(ps + dtc_ref[...]))
        be_ref[...] = jnp.concatenate([_lane_bcast(beta, hd, HEAD) for hd in range(N_HEADS)], axis=-1)
        eg_ref[...] = jnp.concatenate([_lane_bcast(eg, N_HEADS + hd, HEAD) for hd in range(N_HEADS)], axis=-1)


def _sample_state_kernel(s_ref, q_ref, k_ref, v_ref, be_ref, eg_ref, so_ref, o_ref):
    bb = q_ref.shape[0]
    row = lax.broadcasted_iota(jnp.int32, (bb, HEAD), 0)
    erow = lax.broadcasted_iota(jnp.int32, (bb, bb * HEAD), 0)
    elane = lax.broadcasted_iota(jnp.int32, (bb, bb * HEAD), 1)
    diag = (elane // HEAD) == erow
    zeros_pad = jnp.zeros((2 * V7X_SUBLANES - bb, HEAD), F32)
    heads = [slice(hd * HEAD, (hd + 1) * HEAD) for hd in range(N_HEADS)]
    ks_all, qs_all = [], []
    for sl in heads:
        kq = jnp.concatenate([k_ref[:, sl], q_ref[:, sl]], axis=0).astype(BF16)
        hd = sl.start // HEAD
        prods = [jnp.dot(kq, s_ref[tk, hd].astype(BF16), preferred_element_type=F32) for tk in range(bb)]
        ks = prods[0][0:bb]
        qs = prods[0][bb:2 * bb]
        for tk in range(1, bb):
            ks = jnp.where(row == tk, prods[tk][0:bb], ks)
            qs = jnp.where(row == tk, prods[tk][bb:2 * bb], qs)
        ks_all.append(ks)
        qs_all.append(qs)
    outers = []
    for sl, ks, qs in zip(heads, ks_all, qs_all):
        qh = q_ref[:, sl]
        kh = k_ref[:, sl]
        eg = eg_ref[:, sl]
        vn = be_ref[:, sl] * (v_ref[:, sl] - eg * ks)
        o_ref[:, sl] = eg * qs + jnp.sum(qh * kh, axis=-1, keepdims=True) * vn
        vexp = jnp.where(diag, jnp.concatenate([vn] * bb, axis=-1), 0.0)
        outers.append(_dot_tn(jnp.concatenate([kh, zeros_pad], axis=0),
                              jnp.concatenate([vexp, jnp.zeros((2 * V7X_SUBLANES - bb, bb * HEAD), F32)], axis=0)))
    for sl, outer in zip(heads, outers):
        hd = sl.start // HEAD
        eg = eg_ref[:, sl]
        for tk in range(bb):
            so_ref[tk, hd] = (s_ref[tk, hd] * jnp.broadcast_to(eg[tk:tk + 1, :], (HEAD, HEAD))
                              + outer[:, tk * HEAD:(tk + 1) * HEAD])


def _sample_back_kernel(row0, x_ref, mod_ref, o_ref, ya_ref, rest_ref, onw_ref, wob_ref, wo_ref, npost_ref, y_ref):
    n, d = x_ref.shape
    on = _head_rms_gate(o_ref[...], onw_ref[...], rest_ref[:, 0:d])
    yb = jnp.dot(on.astype(BF16), wob_ref[...], preferred_element_type=F32)
    merged = _sigmoid(rest_ref[:, d:2 * d]) * ya_ref[...] + _sigmoid(rest_ref[:, 2 * d:3 * d]) * yb
    post = _rms(jnp.dot(merged.astype(BF16), wo_ref[...], preferred_element_type=F32), npost_ref[...])
    y_ref[...] = x_ref[...] + mod_ref[row0:row0 + n, 2 * d:3 * d] * post


def _sample(x, mod, row0, sa, sb, s0, w):
    n, d = x.shape
    dqkv = 3 * d
    vm = pl.BlockSpec(memory_space=pltpu.VMEM)
    params = pltpu.CompilerParams(vmem_limit_bytes=V7X_VMEM_LIMIT_BYTES)
    row = jax.ShapeDtypeStruct((n, d), F32)
    front_in = (x, mod, sa, sb, w["npre"], w["wmain"], w["wgate"], w["wsmall"], w["caw"], w["cbw"],
                w["alc"], w["dtc"], w["woa"])
    n_blk = (4 * d + dqkv + d) // d

    def whole(shape):
        return pl.BlockSpec(shape, lambda j: (0,) * len(shape))

    front_out = [row, row, row, row, row, row, jax.ShapeDtypeStruct((n, 3 * d), F32),
                 jax.ShapeDtypeStruct(sa.shape, F32), jax.ShapeDtypeStruct(sb.shape, F32)]
    q, k, v, be, eg, ya, rest, nsa, nsb = pl.pallas_call(
        functools.partial(_sample_front_kernel, row0),
        grid=(n_blk,),
        in_specs=[pl.BlockSpec((d, d), lambda j: (0, j)) if a is w["wmain"] else vm for a in front_in],
        out_specs=[whole(o.shape) for o in front_out],
        out_shape=front_out,
        scratch_shapes=[pltpu.VMEM((n, d), BF16), pltpu.VMEM((n_blk, n, d), F32)],
        compiler_params=pltpu.CompilerParams(dimension_semantics=("arbitrary",),
                                             vmem_limit_bytes=V7X_VMEM_LIMIT_BYTES),
        name="sample_front",
    )(*front_in)

    bb = SAMPLE_BLOCK
    tok = pl.BlockSpec((bb, d), lambda i: (i, 0))
    st = pl.BlockSpec((bb, N_HEADS, HEAD, HEAD), lambda i: (i, 0, 0, 0))
    s_new, o = pl.pallas_call(
        _sample_state_kernel,
        grid=(n // bb,),
        in_specs=[st, tok, tok, tok, tok, tok],
        out_specs=[st, tok],
        out_shape=[jax.ShapeDtypeStruct(s0.shape, F32), row],
        compiler_params=pltpu.CompilerParams(dimension_semantics=("arbitrary",),
                                             vmem_limit_bytes=V7X_VMEM_LIMIT_BYTES),
        name="sample_state",
    )(s0, q, k, v, be, eg)

    back_in = (x, mod, o, ya, rest, w["onw"], w["wob"], w["wo"], w["npost"])
    y = pl.pallas_call(
        functools.partial(_sample_back_kernel, row0),
        in_specs=[vm] * len(back_in),
        out_specs=vm,
        out_shape=row,
        compiler_params=params,
        name="sample_back",
    )(*back_in)
    return y, nsa, nsb, s_new


def kernel(x_prompt, x_sample, c_prompt, c_sample, state_conv_a, state_conv_qkv, state_delta, ada_w, ada_b,
           norm_pre, w_in, conv_a_w, conv_b_w, a_log, dt_bias, onorm_w, w_out_a, w_out_b, w_o, norm_post):
    depth = w_in.shape[0]
    assert depth == 1, "single-layer trunk"
    bp, seq, d = x_prompt.shape
    ns = x_sample.shape[0]
    dqkv = 3 * d
    assert d == N_HEADS * HEAD and seq % PROMPT_TILE == 0 and PROMPT_TILE % CHUNK == 0 and 2 * CHUNK == V7X_LANES
    assert ns % SAMPLE_BLOCK == 0 and x_sample.shape[1] == 1

    off_small = 4 * d + dqkv + d
    w0 = w_in.reshape(w_in.shape[1:])
    wmain = w0.astype(BF16)
    wgate = wmain[:, off_small + 2 * N_HEADS:]
    wsmall = jnp.pad(wmain[:, off_small:off_small + 2 * N_HEADS], ((0, 0), (0, V7X_LANES - 2 * N_HEADS)))
    wsmallt = wmain[:, off_small:off_small + 2 * N_HEADS].T
    zeros_h = jnp.zeros((N_HEADS,), F32)
    lane_pad = jnp.zeros((V7X_LANES - 2 * N_HEADS,), F32)
    w = {
        "npre": norm_pre, "npost": norm_post, "onw": onorm_w,
        "wmain": wmain, "wgate": wgate, "wsmall": wsmall, "wsmallt": wsmallt,
        "caw": conv_a_w[0], "cbw": conv_b_w[0],
        "alc": jnp.concatenate([zeros_h, a_log[0], lane_pad])[None, :],
        "dtc": jnp.concatenate([zeros_h, dt_bias[0], lane_pad])[None, :],
        "alr": jnp.concatenate([zeros_h, a_log[0]])[:, None],
        "dtr": jnp.concatenate([zeros_h, dt_bias[0]])[:, None],
        "woa": w_out_a[0].astype(BF16), "wob": w_out_b[0].astype(BF16), "wo": w_o[0].astype(BF16),
    }

    mod = _adaln(jnp.concatenate([c_prompt, c_sample], axis=0), ada_w.reshape(ada_w.shape[1:]), ada_b)
    consts = (w["npre"], w["wmain"], w["wgate"], w["wsmall"], w["wsmallt"], w["caw"], w["cbw"], w["alc"], w["dtc"],
              w["alr"], w["dtr"], w["onw"], w["woa"], w["wob"], w["wo"], w["npost"])
    yp, pca, pcb, pds = _prompt(x_prompt, mod.reshape(mod.shape[0], 1, 3 * d), consts)

    ys, nsa, nsb, sds = _sample(
        x_sample.reshape(ns, d), mod, bp,
        jnp.transpose(state_conv_a.reshape(ns, CONV_A_W - 1, d), (1, 0, 2)),
        jnp.transpose(state_conv_qkv.reshape(ns, CONV_B_W - 1, dqkv), (1, 0, 2)),
        state_delta.reshape(state_delta.shape[1:]), w)

    return (yp, ys.reshape(ns, 1, d), pca.reshape((1,) + pca.shape), pcb.reshape((1,) + pcb.shape),
            pds.reshape((1,) + pds.shape).astype(state_delta.dtype),
            jnp.transpose(nsa, (1, 0, 2)).reshape(1, ns, CONV_A_W - 1, d),
            jnp.transpose(nsb, (1, 0, 2)).reshape(1, ns, CONV_B_W - 1, dqkv),
            sds.reshape((1,) + sds.shape).astype(state_delta.dtype))
```

```python
import functools

import jax
import jax.numpy as jnp
from jax import lax
from jax.experimental import pallas as pl
from jax.experimental.pallas import tpu as pltpu

F32 = jnp.float32
BF16 = jnp.bfloat16

EPS = 1e-6
N_HEADS = 8
HEAD = 128
CONV_A_W = 3
CONV_B_W = 4

V7X_LANES = 128
V7X_SUBLANES = 8
V7X_VMEM_LIMIT_BYTES = 60 * 1024 * 1024

PROMPT_TILE = 256
SIDE_COLS = 256
SIDE_EVERY_PREP = 2
SIDE_EVERY_SOLVE = 16
SIDE_EVERY = 8
CHUNK = 64
SAMPLE_BLOCK = 8

HI = lax.Precision.HIGHEST


NEG_LOG2_E = -1.4426950408889634


def _sigmoid(x):
    return 1.0 / (1.0 + jnp.exp2(x * NEG_LOG2_E))


def _silu(x):
    return x * _sigmoid(x)


def _softplus(x):
    return jnp.maximum(x, 0.0) + jnp.log1p(jnp.exp(-jnp.abs(x)))


def _dot(a, b):
    return jnp.dot(a.astype(BF16), b.astype(BF16), preferred_element_type=F32)


def _dot_nt(a, b):
    return lax.dot_general(a.astype(BF16), b.astype(BF16), (((1,), (1,)), ((), ())),
                           preferred_element_type=F32)


def _dot_tn(a, b):
    return lax.dot_general(a.astype(BF16), b.astype(BF16), (((0,), (0,)), ((), ())),
                           preferred_element_type=F32)


def _dot_exact(a, b):
    return jnp.dot(a, b, precision=HI, preferred_element_type=F32)


def _rms(x, w):
    return x * lax.rsqrt(jnp.mean(x * x, axis=-1, keepdims=True) + EPS) * w


def _head_l2norm(x, scale):
    outs = []
    for h in range(x.shape[1] // HEAD):
        xh = x[:, h * HEAD:(h + 1) * HEAD]
        outs.append(xh * (lax.rsqrt(jnp.sum(xh * xh, axis=-1, keepdims=True) + EPS) * scale))
    return jnp.concatenate(outs, axis=-1)


def _head_rms_gate(o, w, z):
    outs = []
    for h in range(N_HEADS):
        sl = slice(h * HEAD, (h + 1) * HEAD)
        oh = o[:, sl]
        outs.append(oh * lax.rsqrt(jnp.mean(oh * oh, axis=-1, keepdims=True) + EPS) * w * _silu(z[:, sl]))
    return jnp.concatenate(outs, axis=-1)


def _causal_taps(buf_ref, cols, w_ref, width, rows, zero):
    sub = V7X_SUBLANES
    n_tiles = rows // sub
    tiles = [buf_ref[i * sub:(i + 1) * sub, cols] for i in range(n_tiles + 1)]
    width_cols = tiles[0].shape[1]
    row = lax.broadcasted_iota(jnp.int32, (sub, width_cols), 0)
    w_last = jnp.broadcast_to(w_ref[width - 1:width, cols], (sub, width_cols))
    acc = [tiles[i + 1] * w_last for i in range(n_tiles)]
    for j in range(width - 1):
        s = width - 1 - j
        w_j = jnp.broadcast_to(w_ref[j:j + 1, cols], (sub, width_cols))
        rots = [pltpu.roll(tl, s + zero, axis=0) for tl in tiles]
        from_above = row < s
        acc = [acc[i] + jnp.where(from_above, rots[i], rots[i + 1]) * w_j for i in range(n_tiles)]
    return jnp.concatenate(acc, axis=0)


def _lane_bcast(x, col, width):
    return jnp.broadcast_to(x[:, col:col + 1], (x.shape[0], width))


def _adaln_kernel(c_ref, w_ref, b_ref, o_ref):
    o_ref[...] = _dot(_silu(c_ref[...]), w_ref[...]) + b_ref[...]


def _adaln(c_all, ada_w, ada_b):
    n, d = c_all.shape
    cols = ada_w.shape[1]
    return pl.pallas_call(
        _adaln_kernel,
        grid=(cols // d,),
        in_specs=[pl.BlockSpec((n, d), lambda j: (0, 0)),
                  pl.BlockSpec((d, d), lambda j: (0, j)),
                  pl.BlockSpec((1, d), lambda j: (0, j))],
        out_specs=pl.BlockSpec((n, d), lambda j: (0, j)),
        out_shape=jax.ShapeDtypeStruct((n, cols), F32),
        name="adaln",
    )(c_all, ada_w, ada_b)


def _prompt_kernel(x_ref, shift_ref, scale_ref, gate_ref, npre_ref, wmain_ref, wgate_ref, wsmall_ref, wsmallt_ref,
                   caw_ref, cbw_ref, alc_ref, dtc_ref, alr_ref, dtr_ref, onw_ref, woa_ref, wob_ref, wo_ref,
                   npost_ref,
                   y_ref, ca_ref, cb_ref, s_ref,
                   abuf, qkvbuf, s_scr, q_scr, k_scr, v_scr, o_scr, b_scr, gc_scr, gt_scr, gcr_scr,
                   r_scr, aqk_scr, rhs_scr, qd_scr, kd_scr, egt_scr, mp_scr, sb_scr, ob_scr, wu_scr,
                   pa_scr, pr_scr, yap_scr, ya_scr):
    lt = x_ref.shape[1]
    d = x_ref.shape[2]
    dqkv = qkvbuf.shape[1]
    n_chunks = lt // CHUNK
    pad = V7X_SUBLANES
    t = pl.program_id(1)

    @pl.when(t == 0)
    def _():
        abuf[0:pad, :] = jnp.zeros((pad, d), F32)
        qkvbuf[0:pad, :] = jnp.zeros((pad, dqkv), F32)
        s_scr[...] = jnp.zeros(s_scr.shape, F32)

    x = x_ref[0]
    h = _rms(x, npre_ref[...]) * (1.0 + scale_ref[0]) + shift_ref[0]
    hb = h.astype(BF16)

    side_work = []

    def side_proj(dst_ref, dst_off, w_ref, w_off, blk):
        def run():
            dst_ref[:, dst_off + blk * SIDE_COLS:dst_off + (blk + 1) * SIDE_COLS] = jnp.dot(
                hb, w_ref[:, w_off + blk * SIDE_COLS:w_off + (blk + 1) * SIDE_COLS], preferred_element_type=F32)
        return run

    def side_branch_a(blk):
        def run():
            cols = slice(blk * SIDE_COLS, (blk + 1) * SIDE_COLS)
            u = pa_scr[:, d + blk * SIDE_COLS:d + (blk + 1) * SIDE_COLS] * pa_scr[:, cols]
            abuf[pad:pad + lt, cols] = u
            conv = _causal_taps(abuf, cols, caw_ref, CONV_A_W, lt, jnp.minimum(t, 0))
            b_a = pa_scr[:, 2 * d + blk * SIDE_COLS:2 * d + (blk + 1) * SIDE_COLS]
            z_a = pa_scr[:, 3 * d + blk * SIDE_COLS:3 * d + (blk + 1) * SIDE_COLS]
            yap_scr[:, cols] = ((b_a * conv) * _silu(z_a)).astype(BF16)
            ca_ref[0, :, cols] = abuf[pad + lt - (CONV_A_W - 1):pad + lt, cols]
            abuf[0:pad, cols] = abuf[lt:lt + pad, cols]
        return run

    def side_out_a(blk):
        def run():
            cols = slice(blk * SIDE_COLS, (blk + 1) * SIDE_COLS)
            ya_scr[:, cols] = jnp.dot(yap_scr[...], woa_ref[:, cols], preferred_element_type=F32)
        return run

    side_work += [side_proj(pa_scr, 0, wmain_ref, 0, blk) for blk in range(4 * d // SIDE_COLS)]
    side_work += [side_proj(pr_scr, 0, wmain_ref, 4 * d + dqkv, blk) for blk in range(d // SIDE_COLS)]
    side_work += [side_proj(pr_scr, d, wgate_ref, 0, blk) for blk in range(2 * d // SIDE_COLS)]
    side_work += [side_branch_a(blk) for blk in range(d // SIDE_COLS)]
    side_work += [side_out_a(blk) for blk in range(d // SIDE_COLS)]
    side_work.reverse()

    def tick():
        if side_work:
            side_work.pop()()

    def conv_qkv(blk):
        cols = slice(blk * SIDE_COLS, (blk + 1) * SIDE_COLS)
        cq = _silu(_causal_taps(qkvbuf, cols, cbw_ref, CONV_B_W, lt, jnp.minimum(t, 0)))
        cb_ref[0, :, cols] = qkvbuf[pad + lt - (CONV_B_W - 1):pad + lt, cols]
        qkvbuf[0:pad, cols] = qkvbuf[lt:lt + pad, cols]
        dst = slice((blk * SIDE_COLS) % d, (blk * SIDE_COLS) % d + SIDE_COLS)
        if blk * SIDE_COLS < d:
            q_scr[:, dst] = _head_l2norm(cq, HEAD ** -0.5)
        elif blk * SIDE_COLS < 2 * d:
            k_scr[:, dst] = _head_l2norm(cq, 1.0)
        else:
            v_scr[:, dst] = cq

    def proj_qkv(blk):
        cols = slice(blk * SIDE_COLS, (blk + 1) * SIDE_COLS)
        qkvbuf[pad:pad + lt, cols] = jnp.dot(hb, wmain_ref[:, 4 * d + blk * SIDE_COLS:4 * d + (blk + 1) * SIDE_COLS],
                                             preferred_element_type=F32)

    proj_qkv(0)
    for blk in range(dqkv // SIDE_COLS):
        if blk + 1 < dqkv // SIDE_COLS:
            proj_qkv(blk + 1)
        else:
            tick()
        conv_qkv(blk)

    ps = jnp.dot(hb, wsmall_ref[...], preferred_element_type=F32)
    pst = lax.dot_general(wsmallt_ref[...], hb, (((1,), (1,)), ((), ())), preferred_element_type=F32)
    b_scr[...] = _sigmoid(ps)
    g_col = -jnp.exp(alc_ref[...]) * _softplus(ps + dtc_ref[...])
    g_row = -jnp.exp(alr_ref[...]) * _softplus(pst + dtr_ref[...])
    row_in_chunk = lax.broadcasted_iota(jnp.int32, (lt, V7X_LANES), 0) % CHUNK
    gc_col = g_col
    shift = 1
    while shift < CHUNK:
        gc_col = gc_col + jnp.where(row_in_chunk >= shift, pltpu.roll(gc_col, shift, axis=0), 0.0)
        shift *= 2
    gc_scr[...] = gc_col
    gt_scr[...] = jnp.concatenate(
        [jnp.broadcast_to(gc_col[(c + 1) * CHUNK - 1:(c + 1) * CHUNK, :], (CHUNK, V7X_LANES))
         for c in range(n_chunks)], axis=0)
    rj = lax.broadcasted_iota(jnp.int32, (CHUNK, CHUNK), 0)
    cj = lax.broadcasted_iota(jnp.int32, (CHUNK, CHUNK), 1)
    triu = jnp.where(rj <= cj, 1.0, 0.0).astype(F32)
    for c in range(n_chunks):
        gcr_scr[c] = _dot_exact(g_row[:, c * CHUNK:(c + 1) * CHUNK], triu)
    tick()

    causal = rj >= cj
    strict = rj > cj
    eye = jnp.where(rj == cj, 1.0, 0.0).astype(F32)

    nb = n_chunks * N_HEADS
    lane = lax.broadcasted_iota(jnp.int32, (CHUNK, 2 * CHUNK), 1)
    low_half = lane < CHUNK
    eye_left = jnp.where(lane == lax.broadcasted_iota(jnp.int32, (CHUNK, 2 * CHUNK), 0), 1.0, 0.0).astype(BF16)

    for c in range(n_chunks):
        rows = slice(c * CHUNK, (c + 1) * CHUNK)
        beta_blk = b_scr[rows, :]
        gc_blk = gc_scr[rows, :]
        gt_blk = gt_scr[rows, :]
        gcr_blk = gcr_scr[c]
        for hd in range(N_HEADS):
            n = c * N_HEADS + hd
            sl = slice(hd * HEAD, (hd + 1) * HEAD)
            qh = q_scr[rows, sl]
            kh = k_scr[rows, sl]
            vh = v_scr[rows, sl]
            bcol = _lane_bcast(beta_blk, hd, HEAD)
            gcol = _lane_bcast(gc_blk, N_HEADS + hd, HEAD)
            gtot = _lane_bcast(gt_blk, N_HEADS + hd, HEAD)
            grow = gcr_blk[N_HEADS + hd:N_HEADS + hd + 1, :]
            decay = jnp.exp(jnp.where(causal, gcol[:, 0:CHUNK] - grow, -jnp.inf))
            egc = jnp.exp(gcol)
            kb = kh * bcol
            kq = _dot_nt(jnp.concatenate([kb, qh], axis=0), kh)
            xm = -jnp.where(strict, kq[0:CHUNK] * decay, 0.0)
            r_scr[n] = jnp.concatenate([xm, eye], axis=-1)
            aqk_scr[n] = (kq[CHUNK:2 * CHUNK] * decay).astype(BF16)
            rhs_scr[n] = jnp.concatenate([vh * bcol, kb * egc], axis=-1)
            qd_scr[n] = qh * egc
            kd_scr[n] = (kh * jnp.exp(gtot - gcol)).astype(BF16)
            egt_scr[n] = jnp.exp(gtot[0:V7X_SUBLANES, :])
            if n % SIDE_EVERY_PREP == SIDE_EVERY_PREP - 1:
                tick()

    def split_parts(r):
        hi = r.astype(BF16)
        hi32 = hi.astype(F32)
        lo32 = r - hi32
        return hi, hi32, lo32

    m = 1
    while m < CHUNK:
        for n in range(nb):
            r = r_scr[n]
            hi, hi32, lo32 = split_parts(r)
            lhs = jnp.where(low_half, hi32, pltpu.roll(lo32, CHUNK, axis=1)).astype(BF16)
            lo = lo32.astype(BF16)
            him = hi - eye_left
            prod = jnp.dot(jnp.concatenate([lhs, lhs], axis=1), jnp.concatenate([him, him, lo, lo], axis=0),
                           preferred_element_type=F32)
            r_scr[n] = prod + r
            if n % SIDE_EVERY_SOLVE == SIDE_EVERY_SOLVE - 1:
                tick()
        m *= 2
    for n in range(nb):
        hi, hi32, lo32 = split_parts(r_scr[n])
        lhs = jnp.where(low_half, pltpu.roll(lo32, CHUNK, axis=1), hi32).astype(BF16)
        bh, _, bl32 = split_parts(rhs_scr[n])
        bl = bl32.astype(BF16)
        sol = jnp.dot(jnp.concatenate([lhs, lhs], axis=1), jnp.concatenate([bh, bh, bl, bl], axis=0),
                      preferred_element_type=F32)
        wu_scr[n] = jnp.concatenate([sol[:, HEAD:2 * HEAD], sol[:, 0:HEAD]], axis=1).astype(BF16)
        if n % SIDE_EVERY == SIDE_EVERY - 1:
            tick()
    for n in range(nb):
        wu = wu_scr[n]
        kw = lax.dot_general(kd_scr[n], wu, (((0,), (0,)), ((), ())), preferred_element_type=F32)
        aw = jnp.dot(aqk_scr[n], wu, preferred_element_type=F32)
        mp_scr[n, 0:HEAD, :] = kw[:, 0:HEAD].astype(BF16)
        mp_scr[n, HEAD:HEAD + CHUNK, :] = (qd_scr[n] - aw[:, 0:HEAD]).astype(BF16)
        sb_scr[n] = kw[:, HEAD:2 * HEAD]
        ob_scr[n] = aw[:, HEAD:2 * HEAD]
        if n % SIDE_EVERY == SIDE_EVERY - 1:
            tick()

    zero_blk = jnp.zeros((HEAD, HEAD), BF16)
    for c in range(n_chunks):
        rows = slice(c * CHUNK, (c + 1) * CHUNK)
        for hd in range(0, N_HEADS, 2):
            n = c * N_HEADS + hd
            s_pair = [s_scr[hd], s_scr[hd + 1]]
            state_diag = jnp.concatenate(
                [jnp.concatenate([s_pair[0].astype(BF16), zero_blk], axis=1),
                 jnp.concatenate([zero_blk, s_pair[1].astype(BF16)], axis=1)], axis=0)
            res2 = jnp.dot(jnp.concatenate([mp_scr[n], mp_scr[n + 1]], axis=1), state_diag,
                           preferred_element_type=F32)
            for i in range(2):
                res = res2[:, i * HEAD:(i + 1) * HEAD]
                sl = slice((hd + i) * HEAD, (hd + i + 1) * HEAD)
                o_scr[rows, sl] = res[HEAD:HEAD + CHUNK] + ob_scr[n + i]
                s_scr[hd + i] = (s_pair[i] * jnp.concatenate([egt_scr[n + i]] * (HEAD // V7X_SUBLANES), axis=0)
                                 - res[0:HEAD] + sb_scr[n + i])
        tick()
    while side_work:
        tick()

    on = _head_rms_gate(o_scr[...], onw_ref[...], pr_scr[:, 0:d])
    yb = jnp.dot(on.astype(BF16), wob_ref[...], preferred_element_type=F32)
    merged = _sigmoid(pr_scr[:, d:2 * d]) * ya_scr[...] + _sigmoid(pr_scr[:, 2 * d:3 * d]) * yb
    post = _rms(jnp.dot(merged.astype(BF16), wo_ref[...], preferred_element_type=F32), npost_ref[...])
    y_ref[0] = x + gate_ref[0] * post

    @pl.when(t == pl.num_programs(1) - 1)
    def _():
        s_ref[0] = s_scr[...]


def _prompt(x, mod, consts):
    bsz, seq, d = x.shape
    dqkv = 3 * d
    lt = PROMPT_TILE
    pad = V7X_SUBLANES
    nb = (lt // CHUNK) * N_HEADS
    vm = pl.BlockSpec(memory_space=pltpu.VMEM)
    shift_spec, scale_spec, gate_spec = [pl.BlockSpec((1, 1, d), lambda b, t, j=j: (b, 0, j)) for j in range(3)]
    return pl.pallas_call(
        _prompt_kernel,
        grid=(bsz, seq // lt),
        in_specs=[pl.BlockSpec((1, lt, d), lambda b, t: (b, t, 0)), shift_spec, scale_spec, gate_spec]
        + [vm] * len(consts),
        out_specs=[pl.BlockSpec((1, lt, d), lambda b, t: (b, t, 0)),
                   pl.BlockSpec((1, CONV_A_W - 1, d), lambda b, t: (b, 0, 0)),
                   pl.BlockSpec((1, CONV_B_W - 1, dqkv), lambda b, t: (b, 0, 0)),
                   pl.BlockSpec((1, N_HEADS, HEAD, HEAD), lambda b, t: (b, 0, 0, 0))],
        out_shape=[jax.ShapeDtypeStruct((bsz, seq, d), F32),
                   jax.ShapeDtypeStruct((bsz, CONV_A_W - 1, d), F32),
                   jax.ShapeDtypeStruct((bsz, CONV_B_W - 1, dqkv), F32),
                   jax.ShapeDtypeStruct((bsz, N_HEADS, HEAD, HEAD), F32)],
        scratch_shapes=[pltpu.VMEM((pad + lt, d), F32),
                        pltpu.VMEM((pad + lt, dqkv), F32),
                        pltpu.VMEM((N_HEADS, HEAD, HEAD), F32),
                        pltpu.VMEM((lt, d), F32), pltpu.VMEM((lt, d), F32),
                        pltpu.VMEM((lt, d), F32), pltpu.VMEM((lt, d), F32),
                        pltpu.VMEM((lt, V7X_LANES), F32), pltpu.VMEM((lt, V7X_LANES), F32),
                        pltpu.VMEM((lt, V7X_LANES), F32),
                        pltpu.VMEM((lt // CHUNK, 2 * N_HEADS, CHUNK), F32),
                        pltpu.VMEM((nb, CHUNK, 2 * CHUNK), F32),
                        pltpu.VMEM((nb, CHUNK, CHUNK), BF16), pltpu.VMEM((nb, CHUNK, 2 * HEAD), F32),
                        pltpu.VMEM((nb, CHUNK, HEAD), F32), pltpu.VMEM((nb, CHUNK, HEAD), BF16),
                        pltpu.VMEM((nb, V7X_SUBLANES, HEAD), F32), pltpu.VMEM((nb, HEAD + CHUNK, HEAD), BF16),
                        pltpu.VMEM((nb, HEAD, HEAD), F32), pltpu.VMEM((nb, CHUNK, HEAD), F32),
                        pltpu.VMEM((nb, CHUNK, 2 * HEAD), BF16),
                        pltpu.VMEM((lt, 4 * d), F32), pltpu.VMEM((lt, 3 * d), F32),
                        pltpu.VMEM((lt, d), BF16), pltpu.VMEM((lt, d), F32)],
        compiler_params=pltpu.CompilerParams(dimension_semantics=("arbitrary", "arbitrary"),
                                             vmem_limit_bytes=V7X_VMEM_LIMIT_BYTES),
        name="prompt",
    )(x, mod, mod, mod, *consts)


def _sample_front_kernel(row0, x_ref, mod_ref, sa_ref, sb_ref, npre_ref, wmain_ref, wgate_ref, wsmall_ref,
                         caw_ref, cbw_ref, alc_ref, dtc_ref, woa_ref,
                         q_ref, k_ref, v_ref, be_ref, eg_ref, ya_ref, rest_ref, nsa_ref, nsb_ref):
    d = x_ref.shape[1]
    dqkv = 3 * d
    n = x_ref.shape[0]
    x = x_ref[...]
    h = _rms(x, npre_ref[...]) * (1.0 + mod_ref[row0:row0 + n, d:2 * d]) + mod_ref[row0:row0 + n, 0:d]
    hb = h.astype(BF16)

    pa = jnp.dot(hb, wmain_ref[:, 0:4 * d], preferred_element_type=F32)
    u = pa[:, d:2 * d] * pa[:, 0:d]
    conv = u * caw_ref[CONV_A_W - 1:CONV_A_W, :]
    for j in range(CONV_A_W - 1):
        conv = conv + sa_ref[j] * caw_ref[j:j + 1, :]
    ya_ref[...] = jnp.dot(((pa[:, 2 * d:3 * d] * conv) * _silu(pa[:, 3 * d:4 * d])).astype(BF16), woa_ref[...],
                          preferred_element_type=F32)
    for j in range(CONV_A_W - 2):
        nsa_ref[j] = sa_ref[j + 1]
    nsa_ref[CONV_A_W - 2] = u

    pq = jnp.dot(hb, wmain_ref[:, 4 * d:4 * d + dqkv], preferred_element_type=F32)
    cq = pq * cbw_ref[CONV_B_W - 1:CONV_B_W, :]
    for j in range(CONV_B_W - 1):
        cq = cq + sb_ref[j] * cbw_ref[j:j + 1, :]
    cq = _silu(cq)
    for j in range(CONV_B_W - 2):
        nsb_ref[j] = sb_ref[j + 1]
    nsb_ref[CONV_B_W - 2] = pq
    q_ref[...] = _head_l2norm(cq[:, 0:d], HEAD ** -0.5)
    k_ref[...] = _head_l2norm(cq[:, d:2 * d], 1.0)
    v_ref[...] = cq[:, 2 * d:3 * d]

    rest_ref[:, 0:d] = jnp.dot(hb, wmain_ref[:, 4 * d + dqkv:5 * d + dqkv], preferred_element_type=F32)
    rest_ref[:, d:3 * d] = jnp.dot(hb, wgate_ref[...], preferred_element_type=F32)

    ps = jnp.dot(hb, wsmall_ref[...], preferred_element_type=F32)
    beta = _sigmoid(ps)
    eg = jnp.exp(-jnp.exp(alc_ref[...]) * _softplus(ps + dtc_ref[...]))
    be_ref[...] = jnp.concatenate([_lane_bcast(beta, hd, HEAD) for hd in range(N_HEADS)], axis=-1)
    eg_ref[...] = jnp.concatenate([_lane_bcast(eg, N_HEADS + hd, HEAD) for hd in range(N_HEADS)], axis=-1)


def _sample_state_kernel(s_ref, q_ref, k_ref, v_ref, be_ref, eg_ref, so_ref, o_ref):
    bb = q_ref.shape[0]
    row = lax.broadcasted_iota(jnp.int32, (bb, HEAD), 0)
    erow = lax.broadcasted_iota(jnp.int32, (bb, bb * HEAD), 0)
    elane = lax.broadcasted_iota(jnp.int32, (bb, bb * HEAD), 1)
    diag = (elane // HEAD) == erow
    zeros_pad = jnp.zeros((2 * V7X_SUBLANES - bb, HEAD), F32)
    heads = [slice(hd * HEAD, (hd + 1) * HEAD) for hd in range(N_HEADS)]
    ks_all, qs_all = [], []
    for sl in heads:
        kq = jnp.concatenate([k_ref[:, sl], q_ref[:, sl]], axis=0).astype(BF16)
        hd = sl.start // HEAD
        prods = [jnp.dot(kq, s_ref[tk, hd].astype(BF16), preferred_element_type=F32) for tk in range(bb)]
        ks = prods[0][0:bb]
        qs = prods[0][bb:2 * bb]
        for tk in range(1, bb):
            ks = jnp.where(row == tk, prods[tk][0:bb], ks)
            qs = jnp.where(row == tk, prods[tk][bb:2 * bb], qs)
        ks_all.append(ks)
        qs_all.append(qs)
    outers = []
    for sl, ks, qs in zip(heads, ks_all, qs_all):
        qh = q_ref[:, sl]
        kh = k_ref[:, sl]
        eg = eg_ref[:, sl]
        vn = be_ref[:, sl] * (v_ref[:, sl] - eg * ks)
        o_ref[:, sl] = eg * qs + jnp.sum(qh * kh, axis=-1, keepdims=True) * vn
        vexp = jnp.where(diag, jnp.concatenate([vn] * bb, axis=-1), 0.0)
        outers.append(_dot_tn(jnp.concatenate([kh, zeros_pad], axis=0),
                              jnp.concatenate([vexp, jnp.zeros((2 * V7X_SUBLANES - bb, bb * HEAD), F32)], axis=0)))
    for sl, outer in zip(heads, outers):
        hd = sl.start // HEAD
        eg = eg_ref[:, sl]
        for tk in range(bb):
            so_ref[tk, hd] = (s_ref[tk, hd] * jnp.broadcast_to(eg[tk:tk + 1, :], (HEAD, HEAD))
                              + outer[:, tk * HEAD:(tk + 1) * HEAD])


def _sample_back_kernel(row0, x_ref, mod_ref, o_ref, ya_ref, rest_ref, onw_ref, wob_ref, wo_ref, npost_ref, y_ref):
    n, d = x_ref.shape
    on = _head_rms_gate(o_ref[...], onw_ref[...], rest_ref[:, 0:d])
    yb = jnp.dot(on.astype(BF16), wob_ref[...], preferred_element_type=F32)
    merged = _sigmoid(rest_ref[:, d:2 * d]) * ya_ref[...] + _sigmoid(rest_ref[:, 2 * d:3 * d]) * yb
    post = _rms(jnp.dot(merged.astype(BF16), wo_ref[...], preferred_element_type=F32), npost_ref[...])
    y_ref[...] = x_ref[...] + mod_ref[row0:row0 + n, 2 * d:3 * d] * post


def _sample(x, mod, row0, sa, sb, s0, w):
    n, d = x.shape
    dqkv = 3 * d
    vm = pl.BlockSpec(memory_space=pltpu.VMEM)
    params = pltpu.CompilerParams(vmem_limit_bytes=V7X_VMEM_LIMIT_BYTES)
    row = jax.ShapeDtypeStruct((n, d), F32)
    front_in = (x, mod, sa, sb, w["npre"], w["wmain"], w["wgate"], w["wsmall"], w["caw"], w["cbw"],
                w["alc"], w["dtc"], w["woa"])
    q, k, v, be, eg, ya, rest, nsa, nsb = pl.pallas_call(
        functools.partial(_sample_front_kernel, row0),
        in_specs=[vm] * len(front_in),
        out_specs=[vm] * 9,
        out_shape=[row, row, row, row, row, row, jax.ShapeDtypeStruct((n, 3 * d), F32),
                   jax.ShapeDtypeStruct(sa.shape, F32), jax.ShapeDtypeStruct(sb.shape, F32)],
        compiler_params=params,
        name="sample_front",
    )(*front_in)

    bb = SAMPLE_BLOCK
    tok = pl.BlockSpec((bb, d), lambda i: (i, 0))
    st = pl.BlockSpec((bb, N_HEADS, HEAD, HEAD), lambda i: (i, 0, 0, 0))
    s_new, o = pl.pallas_call(
        _sample_state_kernel,
        grid=(n // bb,),
        in_specs=[st, tok, tok, tok, tok, tok],
        out_specs=[st, tok],
        out_shape=[jax.ShapeDtypeStruct(s0.shape, F32), row],
        compiler_params=pltpu.CompilerParams(dimension_semantics=("arbitrary",),
                                             vmem_limit_bytes=V7X_VMEM_LIMIT_BYTES),
        name="sample_state",
    )(s0, q, k, v, be, eg)

    back_in = (x, mod, o, ya, rest, w["onw"], w["wob"], w["wo"], w["npost"])
    y = pl.pallas_call(
        functools.partial(_sample_back_kernel, row0),
        in_specs=[vm] * len(back_in),
        out_specs=vm,
        out_shape=row,
        compiler_params=params,
        name="sample_back",
    )(*back_in)
    return y, nsa, nsb, s_new


def kernel(x_prompt, x_sample, c_prompt, c_sample, state_conv_a, state_conv_qkv, state_delta, ada_w, ada_b,
           norm_pre, w_in, conv_a_w, conv_b_w, a_log, dt_bias, onorm_w, w_out_a, w_out_b, w_o, norm_post):
    depth = w_in.shape[0]
    assert depth == 1, "single-layer trunk"
    bp, seq, d = x_prompt.shape
    ns = x_sample.shape[0]
    dqkv = 3 * d
    assert d == N_HEADS * HEAD and seq % PROMPT_TILE == 0 and PROMPT_TILE % CHUNK == 0 and 2 * CHUNK == V7X_LANES
    assert ns % SAMPLE_BLOCK == 0 and x_sample.shape[1] == 1

    off_small = 4 * d + dqkv + d
    w0 = w_in.reshape(w_in.shape[1:])
    wmain = w0.astype(BF16)
    wgate = wmain[:, off_small + 2 * N_HEADS:]
    wsmall = jnp.pad(wmain[:, off_small:off_small + 2 * N_HEADS], ((0, 0), (0, V7X_LANES - 2 * N_HEADS)))
    wsmallt = wmain[:, off_small:off_small + 2 * N_HEADS].T
    zeros_h = jnp.zeros((N_HEADS,), F32)
    lane_pad = jnp.zeros((V7X_LANES - 2 * N_HEADS,), F32)
    w = {
        "npre": norm_pre, "npost": norm_post, "onw": onorm_w,
        "wmain": wmain, "wgate": wgate, "wsmall": wsmall, "wsmallt": wsmallt,
        "caw": conv_a_w[0], "cbw": conv_b_w[0],
        "alc": jnp.concatenate([zeros_h, a_log[0], lane_pad])[None, :],
        "dtc": jnp.concatenate([zeros_h, dt_bias[0], lane_pad])[None, :],
        "alr": jnp.concatenate([zeros_h, a_log[0]])[:, None],
        "dtr": jnp.concatenate([zeros_h, dt_bias[0]])[:, None],
        "woa": w_out_a[0].astype(BF16), "wob": w_out_b[0].astype(BF16), "wo": w_o[0].astype(BF16),
    }

    mod = _adaln(jnp.concatenate([c_prompt, c_sample], axis=0), ada_w.reshape(ada_w.shape[1:]), ada_b)
    consts = (w["npre"], w["wmain"], w["wgate"], w["wsmall"], w["wsmallt"], w["caw"], w["cbw"], w["alc"], w["dtc"],
              w["alr"], w["dtr"], w["onw"], w["woa"], w["wob"], w["wo"], w["npost"])
    yp, pca, pcb, pds = _prompt(x_prompt, mod.reshape(mod.shape[0], 1, 3 * d), consts)

    ys, nsa, nsb, sds = _sample(
        x_sample.reshape(ns, d), mod, bp,
        jnp.transpose(state_conv_a.reshape(ns, CONV_A_W - 1, d), (1, 0, 2)),
        jnp.transpose(state_conv_qkv.reshape(ns, CONV_B_W - 1, dqkv), (1, 0, 2)),
        state_delta.reshape(state_delta.shape[1:]), w)

    return (yp, ys.reshape(ns, 1, d), pca.reshape((1,) + pca.shape), pcb.reshape((1,) + pcb.shape),
            pds.reshape((1,) + pds.shape).astype(state_delta.dtype),
            jnp.transpose(nsa, (1, 0, 2)).reshape(1, ns, CONV_A_W - 1, d),
            jnp.transpose(nsb, (1, 0, 2)).reshape(1, ns, CONV_B_W - 1, dqkv),
            sds.reshape((1,) + sds.shape).astype(state_delta.dtype))
```
